```python
import jax, jax.numpy as jnp
from jax import lax
import numpy as np

D_MODEL = 1024
BATCH = 8
SEQ = 4096
DEPTH = 1

GRID_W = 64
CTX_LEN = 256
MLA_HEADS = 8
MLA_Q_RANK = 256
MLA_KV_RANK = 128
MLA_NOPE = 64
MLA_ROPE = 32
MLA_V = 64
RET_HEADS = 8
RET_DK = 64
RET_DV = 64
RET_CHUNK = 128
N_EXPERTS = 16
EC_CAPACITY = 2
D_EXPERT = 768
Q_BLOCK = 128
ROPE_BASE = 10000.0
EPS = 1e-6
MIX_WIDTH = MLA_HEADS * MLA_V + RET_HEADS * RET_DV
IN_SIZES = (MLA_Q_RANK, MLA_KV_RANK, MLA_ROPE, RET_HEADS * RET_DK, RET_HEADS * RET_DK, RET_HEADS * RET_DV, RET_HEADS * RET_DV)
IN_SPLITS = tuple(int(s) for s in np.cumsum(IN_SIZES)[:-1])
D_IN = int(sum(IN_SIZES))

kernel_name = 'hybrid_mla_retention_ec_dit'


def rmsnorm(x, g):
    xf = x.astype(jnp.float32)
    y = xf * lax.rsqrt(jnp.mean(xf * xf, axis=-1, keepdims=True) + EPS)
    return (y * g.astype(jnp.float32)).astype(x.dtype)


def rope_2d(x, row, col):
    d = x.shape[-1]
    nf = d // 4
    inv = ROPE_BASE ** (-jnp.arange(nf, dtype=jnp.float32) / nf)
    ang = jnp.concatenate([row[:, None] * inv, col[:, None] * inv], axis=-1)
    cos = jnp.cos(ang)[None, :, None, :].astype(x.dtype)
    sin = jnp.sin(ang)[None, :, None, :].astype(x.dtype)
    x1, x2 = x[..., : d // 2], x[..., d // 2:]
    return jnp.concatenate([x1 * cos - x2 * sin, x1 * sin + x2 * cos], axis=-1)


def mla_qkv(cq, ckv, kr, g_q, g_kv, w_uq, w_ukv, row, col):
    B, L, _ = cq.shape
    q = (rmsnorm(cq, g_q) @ w_uq).reshape(B, L, MLA_HEADS, MLA_NOPE + MLA_ROPE)
    kv = (rmsnorm(ckv, g_kv) @ w_ukv).reshape(B, L, MLA_HEADS, MLA_NOPE + MLA_V)
    q_nope, q_rope = q[..., :MLA_NOPE], q[..., MLA_NOPE:]
    k_nope, v = kv[..., :MLA_NOPE], kv[..., MLA_NOPE:]
    k_rope = kr[:, :, None, :]
    if row is not None:
        q_rope = rope_2d(q_rope, row, col)
        k_rope = rope_2d(k_rope, row, col)
    k_rope = jnp.broadcast_to(k_rope, (B, L, MLA_HEADS, MLA_ROPE))
    return (jnp.concatenate([q_nope, q_rope], axis=-1),
            jnp.concatenate([k_nope, k_rope], axis=-1), v)


def block_attention(q, k, v):
    B, L, H, dk = q.shape
    nb = L // Q_BLOCK
    scale = dk ** -0.5
    qb = q.reshape(B, nb, Q_BLOCK, H, dk).transpose(1, 0, 2, 3, 4)

    def one_block(qblk):
        s = jnp.einsum('bqhd,bkhd->bhqk', qblk, k).astype(jnp.float32) * scale
        p = jax.nn.softmax(s, axis=-1).astype(v.dtype)
        return jnp.einsum('bhqk,bkhd->bqhd', p, v)

    out = lax.map(one_block, qb)
    return out.transpose(1, 0, 2, 3, 4).reshape(B, L, H, v.shape[-1])


def retention_qkv(rq, rk, rv, row, col):
    B, L, _ = rq.shape
    q = rq.reshape(B, L, RET_HEADS, RET_DK)
    k = rk.reshape(B, L, RET_HEADS, RET_DK) * (RET_DK ** -0.5)
    v = rv.reshape(B, L, RET_HEADS, RET_DV)
    if row is not None:
        q = rope_2d(q, row, col)
        k = rope_2d(k, row, col)
    return q, k, v


def retention_chunks(q, k, v, log_gamma, r0, inclusive):
    B, L, H, dk = q.shape
    dv = v.shape[-1]
    C = RET_CHUNK
    nc = L // C
    dt = q.dtype
    idx = jnp.arange(C, dtype=jnp.float32)
    diff = idx[:, None] - idx[None, :]
    mask = (diff >= 0) if inclusive else (diff > 0)
    decay_in = jnp.where(mask[None], jnp.exp(log_gamma[:, None, None] * jnp.maximum(diff, 0.0)[None]), 0.0)
    xi = jnp.exp(log_gamma[None, :] * (idx[:, None] + 1.0))
    zeta = jnp.exp(log_gamma[None, :] * (C - 1.0 - idx)[:, None])
    decay_chunk = jnp.exp(log_gamma * C)[None, :, None, None]
    qc = q.reshape(B, nc, C, H, dk)
    kc = k.reshape(B, nc, C, H, dk)
    vc = v.reshape(B, nc, C, H, dv)
    s = jnp.einsum('bnihd,bnjhd->bnhij', qc, kc) * decay_in.astype(dt)
    y_inner = jnp.einsum('bnhij,bnjhe->bnihe', s, vc)
    u = jnp.einsum('bnjhd,bnjhe->bnhde', kc * zeta[:, :, None].astype(dt), vc).astype(jnp.float32)

    def step(r, u_c):
        return decay_chunk * r + u_c, r

    r_fin, r_prev = lax.scan(step, r0, u.transpose(1, 0, 2, 3, 4))
    r_prev = r_prev.transpose(1, 0, 2, 3, 4).astype(dt)
    y_cross = jnp.einsum('bnihd,bnhde->bnihe', qc * xi[:, :, None].astype(dt), r_prev)
    return (y_inner + y_cross).reshape(B, L, H, dv), r_fin


def bidirectional_retention(q, k, v, qc, kc, vc, lg_f, lg_b):
    B = q.shape[0]
    r0 = jnp.zeros((B, RET_HEADS, RET_DK, RET_DV), jnp.float32)
    flip = lambda t: t[:, ::-1]
    yc_f, rc_f = retention_chunks(qc, kc, vc, lg_f, r0, True)
    y_f, _ = retention_chunks(q, k, v, lg_f, rc_f, True)
    yc_b, rc_b = retention_chunks(flip(qc), flip(kc), flip(vc), lg_b, r0, False)
    y_b, _ = retention_chunks(flip(q), flip(k), flip(v), lg_b, rc_b, False)
    return y_f + flip(y_b), yc_f + flip(yc_b)


def retention_out(y, gate, g_ret):
    B, L, H, dv = y.shape
    yf = y.astype(jnp.float32)
    mu = jnp.mean(yf, axis=-1, keepdims=True)
    var = jnp.mean(jnp.square(yf - mu), axis=-1, keepdims=True)
    yn = ((yf - mu) * lax.rsqrt(var + EPS)).reshape(B, L, H * dv) * g_ret.astype(jnp.float32)
    return yn.astype(y.dtype) * jax.nn.silu(gate)


def hybrid_mixer(h, hc, row, col, w_in, g_q, g_kv, w_uq, w_ukv, exp_f, exp_b, g_ret, w_o, want_ctx):
    B, S, _ = h.shape
    Lc = hc.shape[1]
    cq, ckv, kr, rq, rk, rv, rg = jnp.split(h @ w_in, IN_SPLITS, axis=-1)
    cq_c, ckv_c, kr_c, rq_c, rk_c, rv_c, rg_c = jnp.split(hc @ w_in, IN_SPLITS, axis=-1)
    q_a, k_a, v_a = mla_qkv(cq, ckv, kr, g_q, g_kv, w_uq, w_ukv, row, col)
    q_ac, k_ac, v_ac = mla_qkv(cq_c, ckv_c, kr_c, g_q, g_kv, w_uq, w_ukv, None, None)
    o_att = block_attention(q_a, jnp.concatenate([k_a, k_ac], axis=1), jnp.concatenate([v_a, v_ac], axis=1))
    lg_f = jnp.log1p(-jnp.exp2(-exp_f.astype(jnp.float32)))
    lg_b = jnp.log1p(-jnp.exp2(-exp_b.astype(jnp.float32)))
    q_r, k_r, v_r = retention_qkv(rq, rk, rv, row, col)
    q_rc, k_rc, v_rc = retention_qkv(rq_c, rk_c, rv_c, None, None)
    y_r, y_rc = bidirectional_retention(q_r, k_r, v_r, q_rc, k_rc, v_rc, lg_f, lg_b)
    y_lat = jnp.concatenate([o_att.reshape(B, S, MLA_HEADS * MLA_V), retention_out(y_r, rg, g_ret)], axis=-1) @ w_o
    if not want_ctx:
        return y_lat, None
    o_att_c = block_attention(q_ac, k_ac, v_ac)
    y_ctx = jnp.concatenate([o_att_c.reshape(B, Lc, MLA_HEADS * MLA_V), retention_out(y_rc, rg_c, g_ret)], axis=-1) @ w_o
    return y_lat, y_ctx


def expert_choice_ffn(h, w_router, w_gate, w_up, w_down):
    B, L, D = h.shape
    cap = EC_CAPACITY * L // N_EXPERTS
    aff = jax.nn.softmax((h @ w_router).astype(jnp.float32), axis=-1)
    g, idx = lax.top_k(jnp.swapaxes(aff, 1, 2), cap)
    xs = jax.vmap(lambda hb, ib: hb[ib])(h, idx)
    a = jnp.einsum('becd,edf->becf', xs, w_gate)
    u = jnp.einsum('becd,edf->becf', xs, w_up)
    y = jnp.einsum('becf,efd->becd', jax.nn.silu(a) * u, w_down) * g[..., None].astype(h.dtype)
    return jax.vmap(lambda ib, yb: jnp.zeros((L, D), yb.dtype).at[ib.reshape(-1)].add(yb.reshape(-1, D)))(idx, y)


def setup_inputs(seed: int = 0) -> dict:
    key = jax.random.key(seed)
    ks = jax.random.split(key, 24)
    f32 = jnp.float32

    def nrm(k, shape, scale):
        return jax.random.normal(k, shape, f32) * scale

    def gain(k, shape):
        return 1.0 + 0.05 * jax.random.normal(k, shape, f32)

    L = DEPTH
    head_exp = 5.0 + jnp.arange(RET_HEADS, dtype=f32)
    return {
        'x': nrm(ks[0], (BATCH, SEQ, D_MODEL), 1.0),
        'c': nrm(ks[1], (BATCH, D_MODEL), 1.0),
        'ctx': nrm(ks[2], (BATCH, CTX_LEN, D_MODEL), 1.0),
        'c_ctx': nrm(ks[3], (D_MODEL,), 1.0),
        'w_ada': nrm(ks[4], (L, D_MODEL, 6 * D_MODEL), 0.5 * D_MODEL ** -0.5),
        'b_ada': nrm(ks[5], (L, 6 * D_MODEL), 0.02),
        'g_norm_mix': gain(ks[6], (L, D_MODEL)),
        'g_norm_ffn': gain(ks[7], (L, D_MODEL)),
        'w_in': nrm(ks[8], (L, D_MODEL, D_IN), D_MODEL ** -0.5),
        'g_q_lora': gain(ks[9], (L, MLA_Q_RANK)),
        'g_kv_lora': gain(ks[10], (L, MLA_KV_RANK)),
        'w_uq': nrm(ks[11], (L, MLA_Q_RANK, MLA_HEADS * (MLA_NOPE + MLA_ROPE)), MLA_Q_RANK ** -0.5),
        'w_ukv': nrm(ks[12], (L, MLA_KV_RANK, MLA_HEADS * (MLA_NOPE + MLA_V)), MLA_KV_RANK ** -0.5),
        'ret_exp_fwd': head_exp + 0.1 * jax.random.normal(ks[13], (L, RET_HEADS), f32),
        'ret_exp_bwd': head_exp + 0.1 * jax.random.normal(ks[14], (L, RET_HEADS), f32),
        'g_ret': gain(ks[15], (L, RET_HEADS * RET_DV)),
        'w_o': nrm(ks[16], (L, MIX_WIDTH, D_MODEL), MIX_WIDTH ** -0.5),
        'w_router': nrm(ks[17], (L, D_MODEL, N_EXPERTS), D_MODEL ** -0.5),
        'w_exp_gate': nrm(ks[18], (L, N_EXPERTS, D_MODEL, D_EXPERT), D_MODEL ** -0.5),
        'w_exp_up': nrm(ks[19], (L, N_EXPERTS, D_MODEL, D_EXPERT), D_MODEL ** -0.5),
        'w_exp_down': nrm(ks[20], (L, N_EXPERTS, D_EXPERT, D_MODEL), D_EXPERT ** -0.5),
        'g_final': gain(ks[21], (D_MODEL,)),
    }


def reference(x, c, ctx, c_ctx, w_ada, b_ada, g_norm_mix, g_norm_ffn, w_in, g_q_lora, g_kv_lora, w_uq, w_ukv, ret_exp_fwd, ret_exp_bwd, g_ret, w_o, w_router, w_exp_gate, w_exp_up, w_exp_down, g_final):
    S = x.shape[1]
    rows = S // GRID_W
    row = jnp.repeat(jnp.arange(rows, dtype=jnp.float32), GRID_W)
    col = jnp.tile(jnp.arange(GRID_W, dtype=jnp.float32), rows)
    for layer in range(DEPTH):
        last = layer == DEPTH - 1
        mod = jax.nn.silu(c) @ w_ada[layer] + b_ada[layer]
        sh1, sc1, ga1, sh2, sc2, ga2 = (m[:, None, :] for m in jnp.split(mod, 6, axis=-1))
        mod_c = jax.nn.silu(c_ctx) @ w_ada[layer] + b_ada[layer]
        csh1, csc1, cga1, csh2, csc2, cga2 = jnp.split(mod_c, 6)
        h = rmsnorm(x, g_norm_mix[layer]) * (1.0 + sc1) + sh1
        hc = rmsnorm(ctx, g_norm_mix[layer]) * (1.0 + csc1) + csh1
        y, yc = hybrid_mixer(h, hc, row, col, w_in[layer], g_q_lora[layer], g_kv_lora[layer], w_uq[layer], w_ukv[layer], ret_exp_fwd[layer], ret_exp_bwd[layer], g_ret[layer], w_o[layer], not last)
        x = x + ga1 * y
        h = rmsnorm(x, g_norm_ffn[layer]) * (1.0 + sc2) + sh2
        x = x + ga2 * expert_choice_ffn(h, w_router[layer], w_exp_gate[layer], w_exp_up[layer], w_exp_down[layer])
        if not last:
            ctx = ctx + cga1 * yc
            hc = rmsnorm(ctx, g_norm_ffn[layer]) * (1.0 + csc2) + csh2
            ctx = ctx + cga2 * expert_choice_ffn(hc, w_router[layer], w_exp_gate[layer], w_exp_up[layer], w_exp_down[layer])
    return rmsnorm(x, g_final)
```

```python
import functools

import numpy as np
import jax
import jax.numpy as jnp
from jax import lax
from jax.experimental import pallas as pl
from jax.experimental.pallas import tpu as pltpu

GRID_W = 64
MLA_HEADS = 8
MLA_Q_RANK = 256
MLA_KV_RANK = 128
MLA_NOPE = 64
MLA_ROPE = 32
MLA_V = 64
RET_HEADS = 8
RET_DK = 64
RET_DV = 64
RET_CHUNK = 128
N_EXPERTS = 16
EC_CAPACITY = 2
ROPE_BASE = 10000.0
EPS = 1e-6

LANE = 128
HEAD_TILE = 128
ATTN_TQ = 256
VMEM_LIMIT = 48 * 1024 * 1024
FFN_VMEM_LIMIT = 60 * 1024 * 1024
LOG2E = 1.4426950408889634

F32 = jnp.float32
BF16 = jnp.bfloat16
HIGHEST = lax.Precision.HIGHEST

W_CQ = (0, 256)
W_CKV = (256, 384)
W_KR = (384, 512)
W_RQ = (512, 1024)
W_RK = (1024, 1536)
W_RV = (1536, 2048)
W_RG = (2048, 2560)
D_IN_EXT = 2560


def _cparams(sem, limit=VMEM_LIMIT):
    return pltpu.CompilerParams(dimension_semantics=sem, vmem_limit_bytes=limit)


def _dot(a, b):
    return jnp.dot(a, b, preferred_element_type=F32)


def _dot_nt(a, b):
    return lax.dot_general(a, b, (((1,), (1,)), ((), ())), preferred_element_type=F32)


def _dot_tn(a, b):
    return lax.dot_general(a, b, (((0,), (0,)), ((), ())), preferred_element_type=F32)


def _silu(x):
    return x * jax.nn.sigmoid(x)


def _mla_lane_maps():
    half = MLA_ROPE // 2
    src = np.zeros(HEAD_TILE, np.int32)
    valid = np.zeros(HEAD_TILE, bool)
    src[0:half] = MLA_NOPE + np.arange(half)
    valid[0:half] = True
    src[half:64] = np.arange(64 - half)
    valid[half:64] = True
    src[64:64 + half] = MLA_NOPE + half + np.arange(half)
    valid[64:64 + half] = True
    n_rest = MLA_NOPE - (64 - half)
    src[64 + half:64 + half + n_rest] = (64 - half) + np.arange(n_rest)
    valid[64 + half:64 + half + n_rest] = True
    return src, valid


def _prep_weights(w_in, w_uq, w_ukv):
    d_model = w_in.shape[0]
    half = MLA_ROPE // 2
    src, valid = _mla_lane_maps()
    is_rope = valid & (src >= MLA_NOPE)
    is_nope = valid & (src < MLA_NOPE)

    o = 0
    cq = w_in[:, o:o + MLA_Q_RANK]; o += MLA_Q_RANK
    ckv = w_in[:, o:o + MLA_KV_RANK]; o += MLA_KV_RANK
    kr = w_in[:, o:o + MLA_ROPE]; o += MLA_ROPE
    n_r = RET_HEADS * RET_DK
    rq = w_in[:, o:o + n_r]; o += n_r
    rk = w_in[:, o:o + n_r]; o += n_r
    rv = w_in[:, o:o + RET_HEADS * RET_DV]; o += RET_HEADS * RET_DV
    rg = w_in[:, o:o + RET_HEADS * RET_DV]

    kr_idx = np.where(is_rope, src - MLA_NOPE, 0)
    kr_ext = jnp.where(jnp.asarray(is_rope)[None, :], kr[:, kr_idx], 0.0)

    hd = RET_DK // 2
    lanes = np.arange(n_r)
    tile, l = lanes // LANE, lanes % LANE
    grp = l // hd
    head = 2 * tile + (grp % 2)
    perm = head * RET_DK + (grp // 2) * hd + (l % hd)
    rq_p = rq[:, perm]
    rk_p = rk[:, perm] * (RET_DK ** -0.5)

    w_in_ext = jnp.concatenate([cq, ckv, kr_ext, rq_p, rk_p, rv, rg], axis=1).astype(BF16)
    assert w_in_ext.shape == (d_model, D_IN_EXT)

    dq = MLA_NOPE + MLA_ROPE
    q_cols = (np.arange(MLA_HEADS)[:, None] * dq + src[None, :]).reshape(-1)
    q_valid = np.tile(valid, MLA_HEADS)
    w_uq_ext = jnp.where(jnp.asarray(q_valid)[None, :], w_uq[:, q_cols], 0.0).astype(BF16)

    dkv = MLA_NOPE + MLA_V
    k_cols = (np.arange(MLA_HEADS)[:, None] * dkv + np.where(is_nope, src, 0)[None, :]).reshape(-1)
    k_valid = np.tile(is_nope, MLA_HEADS)
    w_uk_ext = jnp.where(jnp.asarray(k_valid)[None, :], w_ukv[:, k_cols], 0.0)
    v_cols = (np.arange(MLA_HEADS)[:, None] * dkv + MLA_NOPE + np.arange(MLA_V)[None, :]).reshape(-1)
    w_uv = w_ukv[:, v_cols]
    w_ukv_ext = jnp.concatenate([w_uk_ext, w_uv], axis=1).astype(BF16)
    return w_in_ext, w_uq_ext, w_ukv_ext


def _rope_patterns():
    pat = np.zeros((8, LANE), np.float32)
    half = MLA_ROPE // 2
    nf = MLA_ROPE // 4
    inv = ROPE_BASE ** (-np.arange(nf, dtype=np.float64) / nf)
    for base, sign in ((0, -1.0), (64, 1.0)):
        for i in range(half):
            pat[0, base + i] = inv[i % nf]
            pat[1, base + i] = 1.0 if i < nf else 0.0
            pat[2, base + i] = sign
    hd = RET_DK // 2
    nf = RET_DK // 4
    inv = ROPE_BASE ** (-np.arange(nf, dtype=np.float64) / nf)
    for l in range(LANE):
        i = l % hd
        pat[3, l] = inv[i % nf]
        pat[4, l] = 1.0 if i < nf else 0.0
        pat[5, l] = -1.0 if l < 64 else 1.0
    return jnp.asarray(pat)


def _ada_kernel(c_ref, w_ref, b_ref, o_ref):
    s = _silu(c_ref[...])
    o_ref[...] = jnp.dot(s, w_ref[...], preferred_element_type=F32, precision=HIGHEST) + b_ref[...]


def _ada(cc, w_ada, b_ada):
    rows, d = cc.shape
    n = w_ada.shape[1]
    tn = 1024
    return pl.pallas_call(
        _ada_kernel,
        grid=(n // tn,),
        in_specs=[pl.BlockSpec((rows, d), lambda j: (0, 0)),
                  pl.BlockSpec((d, tn), lambda j: (0, j)),
                  pl.BlockSpec((1, tn), lambda j: (0, j))],
        out_specs=pl.BlockSpec((rows, tn), lambda j: (0, j)),
        out_shape=jax.ShapeDtypeStruct((rows, n), F32),
        compiler_params=_cparams(("arbitrary",)),
        name="ada",
    )(cc, w_ada, b_ada.reshape(1, n))


def _rope_kernel(pat_ref, o_ref, *, tm):
    t = pl.program_id(0) * tm + lax.broadcasted_iota(jnp.int32, (tm, LANE), 0)
    row = (t // GRID_W).astype(F32)
    col = (t % GRID_W).astype(F32)
    for k in range(2):
        inv = pat_ref[3 * k:3 * k + 1, :]
        use_row = pat_ref[3 * k + 1:3 * k + 2, :]
        sign = pat_ref[3 * k + 2:3 * k + 3, :]
        ang = jnp.where(use_row > 0.5, row, col) * inv
        active = sign != 0.0
        o_ref[2 * k] = jnp.where(active, jnp.cos(ang), 1.0)
        o_ref[2 * k + 1] = jnp.where(active, sign * jnp.sin(ang), 0.0)


def _rope_tables(seq):
    tm = min(seq, 512)
    return pl.pallas_call(
        functools.partial(_rope_kernel, tm=tm),
        grid=(seq // tm,),
        in_specs=[pl.BlockSpec((8, LANE), lambda i: (0, 0))],
        out_specs=pl.BlockSpec((4, tm, LANE), lambda i: (0, i, 0)),
        out_shape=jax.ShapeDtypeStruct((4, seq, LANE), F32),
        compiler_params=_cparams(("arbitrary",)),
        name="rope",
    )(_rope_patterns())


def _rms(x):
    return x * lax.rsqrt(jnp.mean(x * x, axis=-1, keepdims=True) + EPS)


def _rot(x, cos, sin):
    return x * cos + pltpu.roll(x, 64, axis=1) * sin


def _inproj_kernel(*refs, is_ctx):
    if is_ctx:
        (x_ref, mod_ref, gmix_ref, win_ref, gkv_ref, wukv_ref,
         ka_ref, va_ref, rk_ref, rv_ref) = refs
    else:
        (x_ref, mod_ref, gmix_ref, win_ref, gq_ref, gkv_ref, wuq_ref, wukv_ref, tab_ref,
         qa_ref, ka_ref, va_ref, rq_ref, rk_ref, rv_ref, rg_ref) = refs
    n_k = MLA_HEADS * HEAD_TILE
    h = _rms(x_ref[0]) * gmix_ref[...] * (1.0 + mod_ref[0, 1:2, :]) + mod_ref[0, 0:1, :]
    z = _dot(h.astype(BF16), win_ref[...])
    ckv = _rms(z[:, W_CKV[0]:W_CKV[1]]) * gkv_ref[...]
    kv = _dot(ckv.astype(BF16), wukv_ref[...])
    kr = z[:, W_KR[0]:W_KR[1]]
    if not is_ctx:
        cm, sm, cr, sr = tab_ref[0], tab_ref[1], tab_ref[2], tab_ref[3]
        kr = _rot(kr, cm, sm)
    for hh in range(MLA_HEADS):
        ka_ref[0, :, hh * HEAD_TILE:(hh + 1) * HEAD_TILE] = (
            kv[:, hh * HEAD_TILE:(hh + 1) * HEAD_TILE] + kr).astype(BF16)
    for hp in range(MLA_HEADS // 2):
        va_ref[0, hp, 0] = kv[:, n_k + hp * LANE:n_k + (hp + 1) * LANE].T.astype(BF16)
    rv_ref[0] = z[:, W_RV[0]:W_RV[1]].astype(BF16)
    n_t = (W_RK[1] - W_RK[0]) // LANE
    if is_ctx:
        rk_ref[0] = z[:, W_RK[0]:W_RK[1]].astype(BF16)
        return
    for t in range(n_t):
        rk_ref[0, :, t * LANE:(t + 1) * LANE] = _rot(
            z[:, W_RK[0] + t * LANE:W_RK[0] + (t + 1) * LANE], cr, sr).astype(BF16)
        rq_ref[0, :, t * LANE:(t + 1) * LANE] = _rot(
            z[:, W_RQ[0] + t * LANE:W_RQ[0] + (t + 1) * LANE], cr, sr).astype(BF16)
    rg_ref[0] = z[:, W_RG[0]:W_RG[1]].astype(BF16)
    cq = _rms(z[:, W_CQ[0]:W_CQ[1]]) * gq_ref[...]
    q = _dot(cq.astype(BF16), wuq_ref[...])
    qscale = (MLA_NOPE + MLA_ROPE) ** -0.5 * LOG2E
    for hh in range(MLA_HEADS):
        qh = _rot(q[:, hh * HEAD_TILE:(hh + 1) * HEAD_TILE], cm, sm)
        qa_ref[0, :, hh * HEAD_TILE:(hh + 1) * HEAD_TILE] = (qh * qscale).astype(BF16)


def _inproj(x, mod, gmix, w_in_ext, gq, gkv, w_uq_ext, w_ukv_ext, tabs, *, is_ctx):
    b, s, d = x.shape
    tm = min(s, 512)
    n_k = MLA_HEADS * HEAD_TILE
    n_r = RET_HEADS * RET_DK
    hp = MLA_HEADS // 2
    per_b = mod.shape[0] > 1
    full = lambda shape: pl.BlockSpec(shape, lambda bi, i: tuple(0 for _ in shape))
    tok = lambda n: pl.BlockSpec((1, tm, n), lambda bi, i: (bi, i, 0))
    tok_shape = lambda n: jax.ShapeDtypeStruct((b, s, n), BF16)
    mod_spec = pl.BlockSpec((1, 6, d), (lambda bi, i: (bi, 0, 0)) if per_b else (lambda bi, i: (0, 0, 0)))
    vt_spec = pl.BlockSpec((1, hp, 1, LANE, tm), lambda bi, i: (bi, 0, i, 0, 0))
    vt_shape = jax.ShapeDtypeStruct((b, hp, s // tm, LANE, tm), BF16)
    if is_ctx:
        ins = [x, mod, gmix, w_in_ext, gkv, w_ukv_ext]
        in_specs = [tok(d), mod_spec, full((1, d)), full(w_in_ext.shape), full((1, MLA_KV_RANK)),
                    full(w_ukv_ext.shape)]
        outs = [(tok(n_k), tok_shape(n_k)), (vt_spec, vt_shape)] + [(tok(n_r), tok_shape(n_r))] * 2
    else:
        ins = [x, mod, gmix, w_in_ext, gq, gkv, w_uq_ext, w_ukv_ext, tabs]
        in_specs = [tok(d), mod_spec, full((1, d)), full(w_in_ext.shape), full((1, MLA_Q_RANK)),
                    full((1, MLA_KV_RANK)), full(w_uq_ext.shape), full(w_ukv_ext.shape),
                    pl.BlockSpec((4, tm, LANE), lambda bi, i: (0, i, 0))]
        outs = [(tok(n_k), tok_shape(n_k))] * 2 + [(vt_spec, vt_shape)] + [(tok(n_r), tok_shape(n_r))] * 4
    return pl.pallas_call(
        functools.partial(_inproj_kernel, is_ctx=is_ctx),
        grid=(b, s // tm),
        in_specs=in_specs,
        out_specs=[o[0] for o in outs],
        out_shape=[o[1] for o in outs],
        compiler_params=_cparams(("arbitrary", "arbitrary")),
        name="inproj_ctx" if is_ctx else "inproj",
    )(*ins)


def _attn_kernel(q_ref, kl_ref, kc_ref, vl_ref, vc_ref, o_ref, *, tk):
    tq = q_ref.shape[1]
    n_chunks = kl_ref.shape[1] // tk
    heads = tuple((hh * HEAD_TILE, (hh + 1) * HEAD_TILE) for hh in range(2))

    def scores(k2):
        out = []
        for lo, hi in heads:
            s = _dot_nt(k2[:, lo:hi], q_ref[0, :, lo:hi])
            out.append((s, jnp.max(s, axis=0, keepdims=True)))
        return tuple(out)

    def consume(carry, s_and_max, vt):
        m, l, acc = carry
        s, s_max = s_and_max
        m_new = jnp.maximum(m, s_max)
        alpha = jnp.exp2(m - m_new)
        p = jnp.exp2(s - m_new)
        l = alpha * l + jnp.sum(p, axis=0, keepdims=True)
        acc = alpha * acc + _dot(vt, p.astype(BF16))
        return m_new, l, acc

    def consume_both(carries, s2, vt):
        return tuple(consume(carries[hh], s2[hh], vt) for hh in range(2))

    def body(c, state):
        carries, s_cur = state
        off = pl.multiple_of((c + 1) * tk, tk)
        s_next = scores(kl_ref[0, pl.ds(off, tk), :])
        return consume_both(carries, s_cur, vl_ref[0, 0, c]), s_next

    init = (jnp.full((1, tq), -jnp.inf, F32), jnp.zeros((1, tq), F32), jnp.zeros((LANE, tq), F32))
    carries, s_last = lax.fori_loop(0, n_chunks - 1, body, ((init, init), scores(kl_ref[0, 0:tk, :])))
    s_ctx = scores(kc_ref[0])
    carries = consume_both(carries, s_last, vl_ref[0, 0, n_chunks - 1])
    carries = consume_both(carries, s_ctx, vc_ref[0, 0, 0])
    (_, l_a, acc_a), (_, l_b, acc_b) = carries
    out_t = jnp.concatenate([(acc_a / l_a)[:MLA_V], (acc_b / l_b)[MLA_V:]], axis=0)
    o_ref[0] = out_t.T.astype(BF16)


def _attention(qa, ka, vt, ka_c, vt_c):
    b, s, _ = qa.shape
    lc = ka_c.shape[1]
    hp, n_chunks, _, tk = vt.shape[1:]
    tq = min(s, ATTN_TQ)
    return pl.pallas_call(
        functools.partial(_attn_kernel, tk=tk),
        grid=(b, hp, s // tq),
        in_specs=[pl.BlockSpec((1, tq, 2 * HEAD_TILE), lambda bi, h, i: (bi, i, h)),
                  pl.BlockSpec((1, s, 2 * HEAD_TILE), lambda bi, h, i: (bi, 0, h)),
                  pl.BlockSpec((1, lc, 2 * HEAD_TILE), lambda bi, h, i: (bi, 0, h)),
                  pl.BlockSpec((1, 1, n_chunks, LANE, tk), lambda bi, h, i: (bi, h, 0, 0, 0)),
                  pl.BlockSpec((1, 1, 1, LANE, lc), lambda bi, h, i: (bi, h, 0, 0, 0))],
        out_specs=pl.BlockSpec((1, tq, 2 * MLA_V), lambda bi, h, i: (bi, i, h)),
        out_shape=jax.ShapeDtypeStruct((b, s, MLA_HEADS * MLA_V), BF16),
        compiler_params=_cparams(("arbitrary", "arbitrary", "arbitrary")),
        name="attn",
    )(qa, ka, ka_c, vt, vt_c)


def _ret_kernel(q_ref, k_ref, v_ref, g_ref, kc_ref, vc_ref, eh_ref, el_ref, gret_ref, o_ref, rb_ref):
    c = RET_CHUNK
    n_lat = q_ref.shape[1] // c
    n_ctx = kc_ref.shape[1] // c

    def log_gamma(e):
        return jnp.log1p(-jnp.exp2(-e))

    lgf_a, lgf_b = log_gamma(eh_ref[0, 0:1, :]), log_gamma(eh_ref[0, 1:2, :])
    lgb_a, lgb_b = log_gamma(eh_ref[0, 2:3, :]), log_gamma(eh_ref[0, 3:4, :])
    lgf_l, lgb_l = log_gamma(el_ref[0, 0:1, :]), log_gamma(el_ref[0, 1:2, :])

    ri = lax.broadcasted_iota(jnp.int32, (c, c), 0)
    ci = lax.broadcasted_iota(jnp.int32, (c, c), 1)
    diff = (ri - ci).astype(F32)

    def decay(lgf, lgb):
        return jnp.where(diff >= 0, jnp.exp(lgf * diff), jnp.exp(-lgb * diff))

    d_a, d_b = decay(lgf_a, lgb_a), decay(lgf_b, lgb_b)
    pos = lax.broadcasted_iota(jnp.int32, (c, LANE), 0).astype(F32)
    xi_f = jnp.exp(lgf_l * (pos + 1.0))
    xi_b = jnp.exp(lgb_l * (c - pos))
    zeta_f = jnp.exp(lgf_l * (c - 1.0 - pos))
    zeta_b = jnp.exp(lgb_l * pos)
    hd = RET_DK // 2
    row_is_a = (ri // hd) % 2 == 0
    col_is_a = ci < RET_DV
    gc_f = jnp.exp(jnp.where(row_is_a, lgf_a, lgf_b) * c)
    gc_b = jnp.exp(jnp.where(row_is_a, lgb_a, lgb_b) * c)
    same_head = (row_is_a == col_is_a).astype(F32)
    lane = lax.broadcasted_iota(jnp.int32, (c, LANE), 1)
    qk_is_a = (lane // hd) % 2 == 0
    v_is_a = lane < RET_DV

    def state_update(r, gc, k, v, zeta):
        return gc * r + _dot_tn((k.astype(F32) * zeta).astype(BF16), v)

    def lat(ref, n):
        return ref[0, pl.ds(pl.multiple_of(n * c, c), c), :]

    zero = jnp.zeros((c, LANE), F32)

    r = zero
    for cc in reversed(range(n_ctx)):
        r = state_update(r, gc_b, kc_ref[0, cc * c:(cc + 1) * c, :], vc_ref[0, cc * c:(cc + 1) * c, :], zeta_b)

    def bwd_body(i, r):
        n = n_lat - 1 - i
        rb_ref[n] = r
        return state_update(r, gc_b, lat(k_ref, n), lat(v_ref, n), zeta_b)

    lax.fori_loop(0, n_lat, bwd_body, r)

    r = zero
    for cc in range(n_ctx):
        r = state_update(r, gc_f, kc_ref[0, cc * c:(cc + 1) * c, :], vc_ref[0, cc * c:(cc + 1) * c, :], zeta_f)

    gret = gret_ref[...]

    def fwd_body(n, r):
        q, k, v = lat(q_ref, n), lat(k_ref, n), lat(v_ref, n)
        qz = jnp.zeros_like(q)
        s_a = _dot_nt(jnp.where(qk_is_a, q, qz), k) * d_a
        s_b = _dot_nt(jnp.where(qk_is_a, qz, q), k) * d_b
        vz = jnp.zeros_like(v)
        p2 = jnp.concatenate([s_a, s_b], axis=1).astype(BF16)
        v2 = jnp.concatenate([jnp.where(v_is_a, v, vz), jnp.where(v_is_a, vz, v)], axis=0)
        qf = q.astype(F32)
        q2 = jnp.concatenate([(qf * xi_f).astype(BF16), (qf * xi_b).astype(BF16)], axis=1)
        r2 = jnp.concatenate([(r * same_head).astype(BF16), (rb_ref[n] * same_head).astype(BF16)], axis=0)
        y = _dot(p2, v2) + _dot(q2, r2)
        inv_n = 1.0 / RET_DV
        sum_a = jnp.sum(jnp.where(v_is_a, y, 0.0), axis=-1, keepdims=True)
        sum_b = jnp.sum(jnp.where(v_is_a, 0.0, y), axis=-1, keepdims=True)
        dlt = y - jnp.where(v_is_a, sum_a, sum_b) * inv_n
        sq = dlt * dlt
        var_a = jnp.sum(jnp.where(v_is_a, sq, 0.0), axis=-1, keepdims=True)
        var_b = jnp.sum(jnp.where(v_is_a, 0.0, sq), axis=-1, keepdims=True)
        yn = dlt * lax.rsqrt(jnp.where(v_is_a, var_a, var_b) * inv_n + EPS) * gret
        o_ref[0, pl.ds(pl.multiple_of(n * c, c), c), :] = (yn * _silu(lat(g_ref, n).astype(F32))).astype(BF16)
        return state_update(r, gc_f, k, v, zeta_f)

    lax.fori_loop(0, n_lat, fwd_body, r)


def _retention(rq, rk, rv, rg, rk_c, rv_c, exp_f, exp_b, g_ret):
    b, s, n = rq.shape
    lc = rk_c.shape[1]
    n_t = n // LANE
    ef = exp_f.reshape(n_t, 2)
    eb = exp_b.reshape(n_t, 2)
    eh = jnp.broadcast_to(jnp.concatenate([ef, eb], axis=1)[:, :, None], (n_t, 4, LANE)).astype(F32)
    lane_is_b = (np.arange(LANE) // (RET_DK // 2)) % 2
    el = jnp.stack([ef[:, lane_is_b], eb[:, lane_is_b]], axis=1).astype(F32)
    tok = lambda length: pl.BlockSpec((1, length, LANE), lambda bi, t: (bi, 0, t))
    return pl.pallas_call(
        _ret_kernel,
        grid=(b, n_t),
        in_specs=[tok(s), tok(s), tok(s), tok(s), tok(lc), tok(lc),
                  pl.BlockSpec((1, 4, LANE), lambda bi, t: (t, 0, 0)),
                  pl.BlockSpec((1, 2, LANE), lambda bi, t: (t, 0, 0)),
                  pl.BlockSpec((1, LANE), lambda bi, t: (0, t))],
        out_specs=tok(s),
        out_shape=jax.ShapeDtypeStruct((b, s, n), BF16),
        scratch_shapes=[pltpu.VMEM((s // RET_CHUNK, RET_CHUNK, LANE), F32)],
        compiler_params=_cparams(("arbitrary", "arbitrary")),
        name="ret",
    )(rq, rk, rv, rg, rk_c, rv_c, eh, el, g_ret.reshape(1, n))


def _outproj_kernel(oa_ref, yr_ref, x_ref, mod_ref, wo_ref, gffn_ref, wr_ref, x1_ref, h2_ref, aff_ref):
    n_a = oa_ref.shape[2]
    y = _dot(oa_ref[0], wo_ref[0:n_a, :]) + _dot(yr_ref[0], wo_ref[n_a:, :])
    x1 = x_ref[0] + mod_ref[0, 2:3, :] * y
    x1_ref[0] = x1
    h2 = _rms(x1) * gffn_ref[...] * (1.0 + mod_ref[0, 4:5, :]) + mod_ref[0, 3:4, :]
    h2_ref[0] = h2
    logits = lax.dot_general(wr_ref[...], h2, (((1,), (1,)), ((), ())),
                             preferred_element_type=F32, precision=HIGHEST)
    ex = jnp.exp(logits - jnp.max(logits, axis=0, keepdims=True))
    aff_ref[0] = ex / jnp.sum(ex, axis=0, keepdims=True)


def _outproj(o_att, y_ret, x, mod, w_o, g_ffn, w_router_t):
    b, s, d = x.shape
    tm = min(s, 512)
    n_e = w_router_t.shape[0]
    tok = lambda n: pl.BlockSpec((1, tm, n), lambda bi, i: (bi, i, 0))
    full = lambda shape: pl.BlockSpec(shape, lambda bi, i: tuple(0 for _ in shape))
    return pl.pallas_call(
        _outproj_kernel,
        grid=(b, s // tm),
        in_specs=[tok(o_att.shape[2]), tok(y_ret.shape[2]), tok(d),
                  pl.BlockSpec((1, 6, d), lambda bi, i: (bi, 0, 0)),
                  full(w_o.shape), full((1, d)), full(w_router_t.shape)],
        out_specs=[tok(d), tok(d), pl.BlockSpec((1, n_e, tm), lambda bi, i: (bi, 0, i))],
        out_shape=[jax.ShapeDtypeStruct((b, s, d), F32), jax.ShapeDtypeStruct((b, s, d), F32),
                   jax.ShapeDtypeStruct((b, n_e, s), F32)],
        compiler_params=_cparams(("arbitrary", "arbitrary")),
        name="outproj",
    )(o_att, y_ret, x, mod, w_o, g_ffn, w_router_t)


def _topk_kernel(aff_ref, idx_ref, c_ref, t_ref, *, cap):
    n_e, n_r, _ = c_ref.shape
    assert n_r <= LANE
    a = aff_ref[0]

    def count(mask):
        ones = jnp.where(mask, 1, 0)
        return jnp.sum(jnp.sum(ones, axis=1, keepdims=True), axis=2, keepdims=True)

    def as_float(bits):
        return pltpu.bitcast(bits, F32)

    def search(i, lo):
        cand = lo | jnp.left_shift(jnp.int32(1), 30 - i)
        return jnp.where(count(a >= as_float(cand)) >= cap, cand, lo)

    thr = lax.fori_loop(0, 31, search, jnp.zeros((n_e, 1, 1), jnp.int32))
    above = a >= as_float(thr + 1)
    need = cap - count(above)
    window = (a >= as_float(thr)) & jnp.logical_not(above)
    tok = (lax.broadcasted_iota(jnp.int32, (1, n_r, LANE), 1) * LANE
           + lax.broadcasted_iota(jnp.int32, (1, n_r, LANE), 2))

    def surplus(state):
        return jnp.max(state[1] - need) > 0

    def drop_one(state):
        win, n_win = state
        active = n_win > need
        inside = win > 0.5
        low = jnp.min(jnp.min(jnp.where(inside, a, jnp.inf), axis=1, keepdims=True), axis=2, keepdims=True)
        cand = inside & (a == low)
        last = jnp.max(jnp.max(jnp.where(cand, tok, -1), axis=1, keepdims=True), axis=2, keepdims=True)
        win = jnp.where(cand & (tok == last) & active, 0.0, win)
        return win, n_win - jnp.where(active, 1, 0)

    win, _ = lax.while_loop(surplus, drop_one, (jnp.where(window, 1.0, 0.0), count(window)))
    sel = above | (win > 0.5)

    li = lax.broadcasted_iota(jnp.int32, (LANE, LANE), 0)
    lj = lax.broadcasted_iota(jnp.int32, (LANE, LANE), 1)
    tri = jnp.where(li <= lj, 1.0, 0.0).astype(BF16)
    ones = jnp.ones((LANE, LANE), BF16)
    n_flat = n_e * n_r
    m2 = jnp.where(sel, 1.0, 0.0).astype(BF16).reshape(n_flat, LANE)
    within = _dot(m2, tri)
    tot = _dot(m2, ones)
    pi = lax.broadcasted_iota(jnp.int32, (n_flat, n_flat), 0)
    pj = lax.broadcasted_iota(jnp.int32, (n_flat, n_flat), 1)
    before = jnp.where((pi // n_r == pj // n_r) & (pj < pi), 1.0, 0.0).astype(BF16)
    off = _dot(before, tot.astype(BF16))
    c_ref[...] = (within + off).reshape(n_e, n_r, LANE)
    t_ref[...] = tot.reshape(n_e, n_r, LANE)

    incl = jnp.where(li <= lj, 1.0, 0.0).astype(BF16)[:n_r]
    jcol = lax.broadcasted_iota(jnp.int32, (cap, LANE), 0).astype(F32)
    lane = lax.broadcasted_iota(jnp.int32, (cap, LANE), 1)
    ones8 = jnp.ones((8, LANE), BF16)
    pad = jnp.zeros((LANE - n_r, LANE), F32)

    def per_expert(e, _):
        tot_e = t_ref[e].astype(BF16)
        row_end = _dot_tn(tot_e, incl)[0:1, :]
        row_start = row_end - _dot_tn(tot_e, jnp.where(li == lj, 1.0, 0.0).astype(BF16)[:n_r])[0:1, :]
        full_rows = (row_end <= jcol) & (lane < n_r)
        in_row = (row_start <= jcol) & (row_end > jcol) & (lane < n_r)
        c_pad = jnp.concatenate([c_ref[e], pad], axis=0) if n_r < LANE else c_ref[e]
        c_row = jnp.dot(jnp.where(in_row, 1.0, 0.0), c_pad, preferred_element_type=F32, precision=HIGHEST)
        n_rows = _dot_nt(ones8, jnp.where(full_rows, 1.0, 0.0).astype(BF16))
        n_lanes = _dot_nt(ones8, jnp.where(c_row <= jcol, 1.0, 0.0).astype(BF16))
        idx_ref[0, pl.ds(e, 1), :] = (n_rows[0:1, :] * LANE + n_lanes[0:1, :]).astype(jnp.int32)
        return 0

    lax.fori_loop(0, n_e, per_expert, 0)


def _topk(aff, cap):
    b, n_e, s = aff.shape
    n_r = s // LANE
    aff4 = aff.reshape(b, n_e, n_r, LANE)
    return pl.pallas_call(
        functools.partial(_topk_kernel, cap=cap),
        grid=(b,),
        in_specs=[pl.BlockSpec((1, n_e, n_r, LANE), lambda bi: (bi, 0, 0, 0))],
        out_specs=pl.BlockSpec((1, n_e, cap), lambda bi: (bi, 0, 0)),
        out_shape=jax.ShapeDtypeStruct((b, n_e, cap), jnp.int32),
        scratch_shapes=[pltpu.VMEM((n_e, n_r, LANE), F32), pltpu.VMEM((n_e, n_r, LANE), F32)],
        compiler_params=_cparams(("arbitrary",)),
        name="topk",
    )(aff4)


def _ffn_kernel(idx_ref, aff_ref, h2_hbm, wg_ref, wu_ref, wd_ref, out_hbm,
                h2_v, acc_v, xs_v, y_v, sem, *, tm):
    bi = pl.program_id(0)
    e = pl.program_id(1)
    cap = idx_ref.shape[2]

    @pl.when(e == 0)
    def _():
        cp = pltpu.make_async_copy(h2_hbm.at[bi], h2_v, sem.at[0])
        cp.start()
        acc_v[...] = jnp.zeros_like(acc_v)
        cp.wait()

    for part in range(cap // tm):
        base = part * tm

        def gather(j, _):
            i = idx_ref[0, 0, base + j]
            xs_v[pl.ds(j, 1), :] = h2_v[pl.ds(i, 1), :]
            return 0

        lax.fori_loop(0, tm, gather, 0)
        xb = xs_v[...].astype(BF16)
        a = _dot(xb, wg_ref[0])
        u = _dot(xb, wu_ref[0])
        y_v[...] = _dot((_silu(a) * u).astype(BF16), wd_ref[0])

        def scatter(j, _):
            i = idx_ref[0, 0, base + j]
            g = aff_ref[0, 0, i]
            acc_v[pl.ds(i, 1), :] = acc_v[pl.ds(i, 1), :] + y_v[pl.ds(j, 1), :] * g
            return 0

        lax.fori_loop(0, tm, scatter, 0)

    @pl.when(e == pl.num_programs(1) - 1)
    def _():
        cp = pltpu.make_async_copy(acc_v, out_hbm.at[bi], sem.at[1])
        cp.start()
        cp.wait()


def _ffn(idx, aff, h2, w_gate, w_up, w_down):
    b, s, d = h2.shape
    n_e, _, f = w_gate.shape
    cap = idx.shape[2]
    tm = min(cap, 256)
    smem = lambda n: pl.BlockSpec((1, 1, n), lambda bi, e: (bi * n_e + e, 0, 0), memory_space=pltpu.SMEM)
    idx = idx.reshape(b * n_e, 1, cap)
    aff = aff.reshape(b * n_e, 1, s)
    return pl.pallas_call(
        functools.partial(_ffn_kernel, tm=tm),
        grid=(b, n_e),
        in_specs=[smem(cap), smem(s),
                  pl.BlockSpec(memory_space=pl.ANY),
                  pl.BlockSpec((1, d, f), lambda bi, e: (e, 0, 0)),
                  pl.BlockSpec((1, d, f), lambda bi, e: (e, 0, 0)),
                  pl.BlockSpec((1, f, d), lambda bi, e: (e, 0, 0))],
        out_specs=pl.BlockSpec(memory_space=pl.ANY),
        out_shape=jax.ShapeDtypeStruct((b, s, d), F32),
        scratch_shapes=[pltpu.VMEM((s, d), F32), pltpu.VMEM((s, d), F32),
                        pltpu.VMEM((tm, d), F32), pltpu.VMEM((tm, d), F32),
                        pltpu.SemaphoreType.DMA((2,))],
        compiler_params=_cparams(("arbitrary", "arbitrary"), FFN_VMEM_LIMIT),
        name="ffn",
    )(idx, aff, h2, w_gate, w_up, w_down)


def _final_kernel(x1_ref, acc_ref, mod_ref, g_ref, o_ref):
    x2 = x1_ref[0] + mod_ref[0, 5:6, :] * acc_ref[0]
    o_ref[0] = _rms(x2) * g_ref[...]


def _final(x1, acc, mod, g_final):
    b, s, d = x1.shape
    tm = min(s, 512)
    tok = pl.BlockSpec((1, tm, d), lambda bi, i: (bi, i, 0))
    return pl.pallas_call(
        _final_kernel,
        grid=(b, s // tm),
        in_specs=[tok, tok, pl.BlockSpec((1, 6, d), lambda bi, i: (bi, 0, 0)),
                  pl.BlockSpec((1, d), lambda bi, i: (0, 0))],
        out_specs=tok,
        out_shape=jax.ShapeDtypeStruct((b, s, d), F32),
        compiler_params=_cparams(("arbitrary", "arbitrary")),
        name="final",
    )(x1, acc, mod, g_final.reshape(1, d))


def kernel(x, c, ctx, c_ctx, w_ada, b_ada, g_norm_mix, g_norm_ffn, w_in, g_q_lora, g_kv_lora, w_uq, w_ukv,
           ret_exp_fwd, ret_exp_bwd, g_ret, w_o, w_router, w_exp_gate, w_exp_up, w_exp_down, g_final):
    b, s, d = x.shape
    depth = w_ada.shape[0]
    assert depth == 1, "the context stream update is only needed for depth > 1"
    assert s % LANE == 0 and s % GRID_W == 0
    cap = EC_CAPACITY * s // N_EXPERTS

    rows = -(-(b + 1) // 8) * 8
    cc = jnp.zeros((rows, d), F32).at[:b].set(c).at[b].set(c_ctx)
    mods = _ada(cc, w_ada[0], b_ada[0])
    mod = mods[:b].reshape(b, 6, d)
    mod_c = mods[b:b + 1].reshape(1, 6, d)

    w_in_ext, w_uq_ext, w_ukv_ext = _prep_weights(w_in[0], w_uq[0], w_ukv[0])
    tabs = _rope_tables(s)
    gmix = g_norm_mix[0].reshape(1, d)
    gq = g_q_lora[0].reshape(1, -1)
    gkv = g_kv_lora[0].reshape(1, -1)

    qa, ka, va, rq, rk, rv, rg = _inproj(x, mod, gmix, w_in_ext, gq, gkv, w_uq_ext, w_ukv_ext, tabs, is_ctx=False)
    ka_c, va_c, rk_c, rv_c = _inproj(ctx, mod_c, gmix, w_in_ext, gq, gkv, w_uq_ext, w_ukv_ext, None, is_ctx=True)

    o_att = _attention(qa, ka, va, ka_c, va_c)
    y_ret = _retention(rq, rk, rv, rg, rk_c, rv_c, ret_exp_fwd[0], ret_exp_bwd[0], g_ret[0])

    x1, h2, aff = _outproj(o_att, y_ret, x, mod, w_o[0].astype(BF16), g_norm_ffn[0].reshape(1, d),
                           w_router[0].T)
    idx = _topk(aff, cap)
    acc = _ffn(idx, aff, h2, w_exp_gate[0].astype(BF16), w_exp_up[0].astype(BF16), w_exp_down[0].astype(BF16))
    return _final(x1, acc, mod, g_final)
```

```python
import functools

import numpy as np
import jax
import jax.numpy as jnp
from jax import lax
from jax.experimental import pallas as pl
from jax.experimental.pallas import tpu as pltpu

GRID_W = 64
MLA_HEADS = 8
MLA_Q_RANK = 256
MLA_KV_RANK = 128
MLA_NOPE = 64
MLA_ROPE = 32
MLA_V = 64
RET_HEADS = 8
RET_DK = 64
RET_DV = 64
RET_CHUNK = 128
N_EXPERTS = 16
EC_CAPACITY = 2
ROPE_BASE = 10000.0
EPS = 1e-6

LANE = 128
HEAD_TILE = 128
ATTN_TQ = 256
RET_UNROLL = 4
ROW_TILE = 8
FFN_UNROLL = 8
VMEM_LIMIT = 48 * 1024 * 1024
FFN_VMEM_LIMIT = 60 * 1024 * 1024
LOG2E = 1.4426950408889634

F32 = jnp.float32
BF16 = jnp.bfloat16
HIGHEST = lax.Precision.HIGHEST

W_CQ = (0, 256)
W_CKV = (256, 384)
W_KR = (384, 512)
W_RQ = (512, 1024)
W_RK = (1024, 1536)
W_RV = (1536, 2048)
W_RG = (2048, 2560)
D_IN_EXT = 2560


def _cparams(sem, limit=VMEM_LIMIT):
    return pltpu.CompilerParams(dimension_semantics=sem, vmem_limit_bytes=limit)


def _dot(a, b):
    return jnp.dot(a, b, preferred_element_type=F32)


def _dot_nt(a, b):
    return lax.dot_general(a, b, (((1,), (1,)), ((), ())), preferred_element_type=F32)


def _dot_tn(a, b):
    return lax.dot_general(a, b, (((0,), (0,)), ((), ())), preferred_element_type=F32)


def _silu(x):
    return x * jax.nn.sigmoid(x)


def _mla_lane_maps():
    half = MLA_ROPE // 2
    src = np.zeros(HEAD_TILE, np.int32)
    valid = np.zeros(HEAD_TILE, bool)
    src[0:half] = MLA_NOPE + np.arange(half)
    valid[0:half] = True
    src[half:64] = np.arange(64 - half)
    valid[half:64] = True
    src[64:64 + half] = MLA_NOPE + half + np.arange(half)
    valid[64:64 + half] = True
    n_rest = MLA_NOPE - (64 - half)
    src[64 + half:64 + half + n_rest] = (64 - half) + np.arange(n_rest)
    valid[64 + half:64 + half + n_rest] = True
    return src, valid


def _prep_weights(w_in, w_uq, w_ukv):
    d_model = w_in.shape[0]
    half = MLA_ROPE // 2
    src, valid = _mla_lane_maps()
    is_rope = valid & (src >= MLA_NOPE)
    is_nope = valid & (src < MLA_NOPE)

    o = 0
    cq = w_in[:, o:o + MLA_Q_RANK]; o += MLA_Q_RANK
    ckv = w_in[:, o:o + MLA_KV_RANK]; o += MLA_KV_RANK
    kr = w_in[:, o:o + MLA_ROPE]; o += MLA_ROPE
    n_r = RET_HEADS * RET_DK
    rq = w_in[:, o:o + n_r]; o += n_r
    rk = w_in[:, o:o + n_r]; o += n_r
    rv = w_in[:, o:o + RET_HEADS * RET_DV]; o += RET_HEADS * RET_DV
    rg = w_in[:, o:o + RET_HEADS * RET_DV]

    kr_idx = np.where(is_rope, src - MLA_NOPE, 0)
    kr_ext = jnp.where(jnp.asarray(is_rope)[None, :], kr[:, kr_idx], 0.0)

    hd = RET_DK // 2
    lanes = np.arange(n_r)
    tile, l = lanes // LANE, lanes % LANE
    grp = l // hd
    head = 2 * tile + (grp % 2)
    perm = head * RET_DK + (grp // 2) * hd + (l % hd)
    rq_p = rq[:, perm]
    rk_p = rk[:, perm] * (RET_DK ** -0.5)

    w_in_ext = jnp.concatenate([cq, ckv, kr_ext, rq_p, rk_p, rv, rg], axis=1).astype(BF16)
    assert w_in_ext.shape == (d_model, D_IN_EXT)

    dq = MLA_NOPE + MLA_ROPE
    q_cols = (np.arange(MLA_HEADS)[:, None] * dq + src[None, :]).reshape(-1)
    q_valid = np.tile(valid, MLA_HEADS)
    w_uq_ext = jnp.where(jnp.asarray(q_valid)[None, :], w_uq[:, q_cols], 0.0).astype(BF16)

    dkv = MLA_NOPE + MLA_V
    k_cols = (np.arange(MLA_HEADS)[:, None] * dkv + np.where(is_nope, src, 0)[None, :]).reshape(-1)
    k_valid = np.tile(is_nope, MLA_HEADS)
    w_uk_ext = jnp.where(jnp.asarray(k_valid)[None, :], w_ukv[:, k_cols], 0.0)
    v_cols = (np.arange(MLA_HEADS)[:, None] * dkv + MLA_NOPE + np.arange(MLA_V)[None, :]).reshape(-1)
    w_uv = w_ukv[:, v_cols]
    w_ukv_ext = jnp.concatenate([w_uk_ext, w_uv], axis=1).astype(BF16)
    return w_in_ext, w_uq_ext, w_ukv_ext


def _rope_patterns():
    pat = np.zeros((8, LANE), np.float32)
    half = MLA_ROPE // 2
    nf = MLA_ROPE // 4
    inv = ROPE_BASE ** (-np.arange(nf, dtype=np.float64) / nf)
    for base, sign in ((0, -1.0), (64, 1.0)):
        for i in range(half):
            pat[0, base + i] = inv[i % nf]
            pat[1, base + i] = 1.0 if i < nf else 0.0
            pat[2, base + i] = sign
    hd = RET_DK // 2
    nf = RET_DK // 4
    inv = ROPE_BASE ** (-np.arange(nf, dtype=np.float64) / nf)
    for l in range(LANE):
        i = l % hd
        pat[3, l] = inv[i % nf]
        pat[4, l] = 1.0 if i < nf else 0.0
        pat[5, l] = -1.0 if l < 64 else 1.0
    return jnp.asarray(pat)


def _ada_kernel(c_ref, w_ref, b_ref, o_ref):
    s = _silu(c_ref[...])
    o_ref[...] = jnp.dot(s, w_ref[...], preferred_element_type=F32, precision=HIGHEST) + b_ref[...]


def _ada(cc, w_ada, b_ada):
    rows, d = cc.shape
    n = w_ada.shape[1]
    tn = 1024
    return pl.pallas_call(
        _ada_kernel,
        grid=(n // tn,),
        in_specs=[pl.BlockSpec((rows, d), lambda j: (0, 0)),
                  pl.BlockSpec((d, tn), lambda j: (0, j)),
                  pl.BlockSpec((1, tn), lambda j: (0, j))],
        out_specs=pl.BlockSpec((rows, tn), lambda j: (0, j)),
        out_shape=jax.ShapeDtypeStruct((rows, n), F32),
        compiler_params=_cparams(("arbitrary",)),
        name="ada",
    )(cc, w_ada, b_ada.reshape(1, n))


def _rope_kernel(pat_ref, o_ref, *, tm):
    t = pl.program_id(0) * tm + lax.broadcasted_iota(jnp.int32, (tm, LANE), 0)
    row = (t // GRID_W).astype(F32)
    col = (t % GRID_W).astype(F32)
    for k in range(2):
        inv = pat_ref[3 * k:3 * k + 1, :]
        use_row = pat_ref[3 * k + 1:3 * k + 2, :]
        sign = pat_ref[3 * k + 2:3 * k + 3, :]
        ang = jnp.where(use_row > 0.5, row, col) * inv
        active = sign != 0.0
        o_ref[2 * k] = jnp.where(active, jnp.cos(ang), 1.0)
        o_ref[2 * k + 1] = jnp.where(active, sign * jnp.sin(ang), 0.0)


def _rope_tables(seq):
    tm = min(seq, 512)
    return pl.pallas_call(
        functools.partial(_rope_kernel, tm=tm),
        grid=(seq // tm,),
        in_specs=[pl.BlockSpec((8, LANE), lambda i: (0, 0))],
        out_specs=pl.BlockSpec((4, tm, LANE), lambda i: (0, i, 0)),
        out_shape=jax.ShapeDtypeStruct((4, seq, LANE), F32),
        compiler_params=_cparams(("arbitrary",)),
        name="rope",
    )(_rope_patterns())


def _rms(x):
    return x * lax.rsqrt(jnp.mean(x * x, axis=-1, keepdims=True) + EPS)


def _rot(x, cos, sin):
    return x * cos + pltpu.roll(x, 64, axis=1) * sin


def _inproj_kernel(*refs, is_ctx):
    if is_ctx:
        (x_ref, mod_ref, gmix_ref, win_ref, gkv_ref, wukv_ref,
         ka_ref, va_ref, rk_ref, rv_ref) = refs
    else:
        (x_ref, mod_ref, gmix_ref, win_ref, gq_ref, gkv_ref, wuq_ref, wukv_ref, tab_ref,
         qa_ref, ka_ref, va_ref, rq_ref, rk_ref, rv_ref, rg_ref) = refs
    n_k = MLA_HEADS * HEAD_TILE
    h = _rms(x_ref[0]) * gmix_ref[...] * (1.0 + mod_ref[0, 1:2, :]) + mod_ref[0, 0:1, :]
    z = _dot(h.astype(BF16), win_ref[...])
    ckv = _rms(z[:, W_CKV[0]:W_CKV[1]]) * gkv_ref[...]
    kv = _dot(ckv.astype(BF16), wukv_ref[...])
    kr = z[:, W_KR[0]:W_KR[1]]
    if not is_ctx:
        cm, sm, cr, sr = tab_ref[0], tab_ref[1], tab_ref[2], tab_ref[3]
        kr = _rot(kr, cm, sm)
    for hh in range(MLA_HEADS):
        ka_ref[0, :, hh * HEAD_TILE:(hh + 1) * HEAD_TILE] = (
            kv[:, hh * HEAD_TILE:(hh + 1) * HEAD_TILE] + kr).astype(BF16)
    for hp in range(MLA_HEADS // 2):
        va_ref[0, hp, 0] = kv[:, n_k + hp * LANE:n_k + (hp + 1) * LANE].T.astype(BF16)
    rv_ref[0] = z[:, W_RV[0]:W_RV[1]].astype(BF16)
    n_t = (W_RK[1] - W_RK[0]) // LANE
    if is_ctx:
        rk_ref[0] = z[:, W_RK[0]:W_RK[1]].astype(BF16)
        return
    for t in range(n_t):
        rk_ref[0, :, t * LANE:(t + 1) * LANE] = _rot(
            z[:, W_RK[0] + t * LANE:W_RK[0] + (t + 1) * LANE], cr, sr).astype(BF16)
        rq_ref[0, :, t * LANE:(t + 1) * LANE] = _rot(
            z[:, W_RQ[0] + t * LANE:W_RQ[0] + (t + 1) * LANE], cr, sr).astype(BF16)
    rg_ref[0] = z[:, W_RG[0]:W_RG[1]].astype(BF16)
    cq = _rms(z[:, W_CQ[0]:W_CQ[1]]) * gq_ref[...]
    q = _dot(cq.astype(BF16), wuq_ref[...])
    qscale = (MLA_NOPE + MLA_ROPE) ** -0.5 * LOG2E
    for hh in range(MLA_HEADS):
        qh = _rot(q[:, hh * HEAD_TILE:(hh + 1) * HEAD_TILE], cm, sm)
        qa_ref[0, :, hh * HEAD_TILE:(hh + 1) * HEAD_TILE] = (qh * qscale).astype(BF16)


def _inproj(x, mod, gmix, w_in_ext, gq, gkv, w_uq_ext, w_ukv_ext, tabs, *, is_ctx):
    b, s, d = x.shape
    tm = min(s, 512)
    n_k = MLA_HEADS * HEAD_TILE
    n_r = RET_HEADS * RET_DK
    hp = MLA_HEADS // 2
    per_b = mod.shape[0] > 1
    full = lambda shape: pl.BlockSpec(shape, lambda bi, i: tuple(0 for _ in shape))
    tok = lambda n: pl.BlockSpec((1, tm, n), lambda bi, i: (bi, i, 0))
    tok_shape = lambda n: jax.ShapeDtypeStruct((b, s, n), BF16)
    mod_spec = pl.BlockSpec((1, 6, d), (lambda bi, i: (bi, 0, 0)) if per_b else (lambda bi, i: (0, 0, 0)))
    vt_spec = pl.BlockSpec((1, hp, 1, LANE, tm), lambda bi, i: (bi, 0, i, 0, 0))
    vt_shape = jax.ShapeDtypeStruct((b, hp, s // tm, LANE, tm), BF16)
    if is_ctx:
        ins = [x, mod, gmix, w_in_ext, gkv, w_ukv_ext]
        in_specs = [tok(d), mod_spec, full((1, d)), full(w_in_ext.shape), full((1, MLA_KV_RANK)),
                    full(w_ukv_ext.shape)]
        outs = [(tok(n_k), tok_shape(n_k)), (vt_spec, vt_shape)] + [(tok(n_r), tok_shape(n_r))] * 2
    else:
        ins = [x, mod, gmix, w_in_ext, gq, gkv, w_uq_ext, w_ukv_ext, tabs]
        in_specs = [tok(d), mod_spec, full((1, d)), full(w_in_ext.shape), full((1, MLA_Q_RANK)),
                    full((1, MLA_KV_RANK)), full(w_uq_ext.shape), full(w_ukv_ext.shape),
                    pl.BlockSpec((4, tm, LANE), lambda bi, i: (0, i, 0))]
        outs = [(tok(n_k), tok_shape(n_k))] * 2 + [(vt_spec, vt_shape)] + [(tok(n_r), tok_shape(n_r))] * 4
    return pl.pallas_call(
        functools.partial(_inproj_kernel, is_ctx=is_ctx),
        grid=(b, s // tm),
        in_specs=in_specs,
        out_specs=[o[0] for o in outs],
        out_shape=[o[1] for o in outs],
        compiler_params=_cparams(("arbitrary", "arbitrary")),
        name="inproj_ctx" if is_ctx else "inproj",
    )(*ins)


def _attn_kernel(q_ref, kl_ref, kc_ref, vl_ref, vc_ref, o_ref, *, tk):
    tq = q_ref.shape[1]
    n_chunks = kl_ref.shape[1] // tk
    assert n_chunks >= 2
    heads = tuple((hh * HEAD_TILE, (hh + 1) * HEAD_TILE) for hh in range(2))

    def scores(k2):
        out = []
        for lo, hi in heads:
            s = _dot_nt(k2[:, lo:hi], q_ref[0, :, lo:hi])
            out.append((s, jnp.max(s, axis=0, keepdims=True)))
        return tuple(out)

    def consume(carry, s_and_max, vt):
        m, l, acc = carry
        s, s_max = s_and_max
        m_new = jnp.maximum(m, s_max)
        alpha = jnp.exp2(m - m_new)
        p = jnp.exp2(s - m_new)
        l = alpha * l + jnp.sum(p, axis=0, keepdims=True)
        acc = alpha * acc + _dot(vt, p.astype(BF16))
        return m_new, l, acc

    def consume_both(carries, s2, vt):
        return tuple(consume(carries[hh], s2[hh], vt) for hh in range(2))

    def body(c, state):
        carries, s_cur = state
        off = pl.multiple_of((c + 1) * tk, tk)
        s_next = scores(kl_ref[0, pl.ds(off, tk), :])
        return consume_both(carries, s_cur, vl_ref[0, 0, c]), s_next

    init = (jnp.full((1, tq), -jnp.inf, F32), jnp.zeros((1, tq), F32), jnp.zeros((LANE, tq), F32))
    carries, s_last = lax.fori_loop(0, n_chunks - 1, body, ((init, init), scores(kl_ref[0, 0:tk, :])))
    s_ctx = scores(kc_ref[0])
    carries = consume_both(carries, s_last, vl_ref[0, 0, n_chunks - 1])
    carries = consume_both(carries, s_ctx, vc_ref[0, 0, 0])
    (_, l_a, acc_a), (_, l_b, acc_b) = carries
    out_t = jnp.concatenate([(acc_a / l_a)[:MLA_V], (acc_b / l_b)[MLA_V:]], axis=0)
    o_ref[0] = out_t.T.astype(BF16)


def _attention(qa, ka, vt, ka_c, vt_c):
    b, s, _ = qa.shape
    lc = ka_c.shape[1]
    hp, n_chunks, _, tk = vt.shape[1:]
    tq = min(s, ATTN_TQ)
    return pl.pallas_call(
        functools.partial(_attn_kernel, tk=tk),
        grid=(b, hp, s // tq),
        in_specs=[pl.BlockSpec((1, tq, 2 * HEAD_TILE), lambda bi, h, i: (bi, i, h)),
                  pl.BlockSpec((1, s, 2 * HEAD_TILE), lambda bi, h, i: (bi, 0, h)),
                  pl.BlockSpec((1, lc, 2 * HEAD_TILE), lambda bi, h, i: (bi, 0, h)),
                  pl.BlockSpec((1, 1, n_chunks, LANE, tk), lambda bi, h, i: (bi, h, 0, 0, 0)),
                  pl.BlockSpec((1, 1, 1, LANE, lc), lambda bi, h, i: (bi, h, 0, 0, 0))],
        out_specs=pl.BlockSpec((1, tq, 2 * MLA_V), lambda bi, h, i: (bi, i, h)),
        out_shape=jax.ShapeDtypeStruct((b, s, MLA_HEADS * MLA_V), BF16),
        compiler_params=_cparams(("arbitrary", "arbitrary", "arbitrary")),
        name="attn",
    )(qa, ka, ka_c, vt, vt_c)


def _ret_kernel(q_ref, k_ref, v_ref, g_ref, kc_ref, vc_ref, eh_ref, el_ref, gret_ref, o_ref, rb_ref):
    c = RET_CHUNK
    n_lat = q_ref.shape[1] // c
    n_ctx = kc_ref.shape[1] // c

    def log_gamma(e):
        return jnp.log1p(-jnp.exp2(-e))

    lgf_a, lgf_b = log_gamma(eh_ref[0, 0:1, :]), log_gamma(eh_ref[0, 1:2, :])
    lgb_a, lgb_b = log_gamma(eh_ref[0, 2:3, :]), log_gamma(eh_ref[0, 3:4, :])
    lgf_l, lgb_l = log_gamma(el_ref[0, 0:1, :]), log_gamma(el_ref[0, 1:2, :])

    ri = lax.broadcasted_iota(jnp.int32, (c, c), 0)
    ci = lax.broadcasted_iota(jnp.int32, (c, c), 1)
    diff = (ri - ci).astype(F32)

    def decay(lgf, lgb):
        return jnp.where(diff >= 0, jnp.exp(lgf * diff), jnp.exp(-lgb * diff))

    d_a, d_b = decay(lgf_a, lgb_a), decay(lgf_b, lgb_b)
    pos = lax.broadcasted_iota(jnp.int32, (c, LANE), 0).astype(F32)
    xi_f = jnp.exp(lgf_l * (pos + 1.0))
    xi_b = jnp.exp(lgb_l * (c - pos))
    zeta_f = jnp.exp(lgf_l * (c - 1.0 - pos))
    zeta_b = jnp.exp(lgb_l * pos)
    hd = RET_DK // 2
    row_is_a = (ri // hd) % 2 == 0
    col_is_a = ci < RET_DV
    gc_f = jnp.exp(jnp.where(row_is_a, lgf_a, lgf_b) * c)
    gc_b = jnp.exp(jnp.where(row_is_a, lgb_a, lgb_b) * c)
    same_head = (row_is_a == col_is_a).astype(F32)
    lane = lax.broadcasted_iota(jnp.int32, (c, LANE), 1)
    qk_is_a = (lane // hd) % 2 == 0
    v_is_a = lane < RET_DV

    def state_update(r, gc, k, v, zeta):
        return gc * r + _dot_tn((k.astype(F32) * zeta).astype(BF16), v)

    def lat(ref, n):
        return ref[0, pl.ds(pl.multiple_of(n * c, c), c), :]

    zero = jnp.zeros((c, LANE), F32)

    r = zero
    for cc in reversed(range(n_ctx)):
        r = state_update(r, gc_b, kc_ref[0, cc * c:(cc + 1) * c, :], vc_ref[0, cc * c:(cc + 1) * c, :], zeta_b)

    def key_value_sum(k, v, zeta):
        return _dot_tn((k.astype(F32) * zeta).astype(BF16), v)

    def bwd_body(i, r):
        ns = [n_lat - 1 - (i * RET_UNROLL + u) for u in range(RET_UNROLL)]
        sums = [key_value_sum(lat(k_ref, n), lat(v_ref, n), zeta_b) for n in ns]
        for n, kv in zip(ns, sums):
            rb_ref[n] = r
            r = gc_b * r + kv
        return r

    assert n_lat % RET_UNROLL == 0
    lax.fori_loop(0, n_lat // RET_UNROLL, bwd_body, r)

    r = zero
    for cc in range(n_ctx):
        r = state_update(r, gc_f, kc_ref[0, cc * c:(cc + 1) * c, :], vc_ref[0, cc * c:(cc + 1) * c, :], zeta_f)

    gret = gret_ref[...]

    def fwd_body(i, r):
        ns = [i * RET_UNROLL + u for u in range(RET_UNROLL)]
        sums = [key_value_sum(lat(k_ref, n), lat(v_ref, n), zeta_f) for n in ns]
        for n, kv in zip(ns, sums):
            chunk_out(n, r)
            r = gc_f * r + kv
        return r

    def chunk_out(n, r):
        q, k, v = lat(q_ref, n), lat(k_ref, n), lat(v_ref, n)
        qz = jnp.zeros_like(q)
        s_a = _dot_nt(jnp.where(qk_is_a, q, qz), k) * d_a
        s_b = _dot_nt(jnp.where(qk_is_a, qz, q), k) * d_b
        vz = jnp.zeros_like(v)
        p2 = jnp.concatenate([s_a, s_b], axis=1).astype(BF16)
        v2 = jnp.concatenate([jnp.where(v_is_a, v, vz), jnp.where(v_is_a, vz, v)], axis=0)
        qf = q.astype(F32)
        q2 = jnp.concatenate([(qf * xi_f).astype(BF16), (qf * xi_b).astype(BF16)], axis=1)
        r2 = jnp.concatenate([(r * same_head).astype(BF16), (rb_ref[n] * same_head).astype(BF16)], axis=0)
        y = _dot(p2, v2) + _dot(q2, r2)
        inv_n = 1.0 / RET_DV
        sum_a = jnp.sum(jnp.where(v_is_a, y, 0.0), axis=-1, keepdims=True)
        sum_b = jnp.sum(jnp.where(v_is_a, 0.0, y), axis=-1, keepdims=True)
        dlt = y - jnp.where(v_is_a, sum_a, sum_b) * inv_n
        sq = dlt * dlt
        var_a = jnp.sum(jnp.where(v_is_a, sq, 0.0), axis=-1, keepdims=True)
        var_b = jnp.sum(jnp.where(v_is_a, 0.0, sq), axis=-1, keepdims=True)
        yn = dlt * lax.rsqrt(jnp.where(v_is_a, var_a, var_b) * inv_n + EPS) * gret
        o_ref[0, pl.ds(pl.multiple_of(n * c, c), c), :] = (yn * _silu(lat(g_ref, n).astype(F32))).astype(BF16)

    lax.fori_loop(0, n_lat // RET_UNROLL, fwd_body, r)


def _retention(rq, rk, rv, rg, rk_c, rv_c, exp_f, exp_b, g_ret):
    b, s, n = rq.shape
    lc = rk_c.shape[1]
    n_t = n // LANE
    ef = exp_f.reshape(n_t, 2)
    eb = exp_b.reshape(n_t, 2)
    eh = jnp.broadcast_to(jnp.concatenate([ef, eb], axis=1)[:, :, None], (n_t, 4, LANE)).astype(F32)
    lane_is_b = (np.arange(LANE) // (RET_DK // 2)) % 2
    el = jnp.stack([ef[:, lane_is_b], eb[:, lane_is_b]], axis=1).astype(F32)
    tok = lambda length: pl.BlockSpec((1, length, LANE), lambda bi, t: (bi, 0, t))
    return pl.pallas_call(
        _ret_kernel,
        grid=(b, n_t),
        in_specs=[tok(s), tok(s), tok(s), tok(s), tok(lc), tok(lc),
                  pl.BlockSpec((1, 4, LANE), lambda bi, t: (t, 0, 0)),
                  pl.BlockSpec((1, 2, LANE), lambda bi, t: (t, 0, 0)),
                  pl.BlockSpec((1, LANE), lambda bi, t: (0, t))],
        out_specs=tok(s),
        out_shape=jax.ShapeDtypeStruct((b, s, n), BF16),
        scratch_shapes=[pltpu.VMEM((s // RET_CHUNK, RET_CHUNK, LANE), F32)],
        compiler_params=_cparams(("arbitrary", "arbitrary")),
        name="ret",
    )(rq, rk, rv, rg, rk_c, rv_c, eh, el, g_ret.reshape(1, n))


def _outproj_kernel(oa_ref, yr_ref, x_ref, mod_ref, wo_ref, gffn_ref, wr_ref, x1_ref, h2_ref, aff_ref):
    n_a = oa_ref.shape[2]
    y = _dot(oa_ref[0], wo_ref[0:n_a, :]) + _dot(yr_ref[0], wo_ref[n_a:, :])
    x1 = x_ref[0] + mod_ref[0, 2:3, :] * y
    x1_ref[0] = x1
    h2 = _rms(x1) * gffn_ref[...] * (1.0 + mod_ref[0, 4:5, :]) + mod_ref[0, 3:4, :]
    h2_ref[0] = h2
    logits = lax.dot_general(wr_ref[...], h2, (((1,), (1,)), ((), ())),
                             preferred_element_type=F32, precision=HIGHEST)
    ex = jnp.exp(logits - jnp.max(logits, axis=0, keepdims=True))
    aff_ref[0] = ex / jnp.sum(ex, axis=0, keepdims=True)


def _outproj(o_att, y_ret, x, mod, w_o, g_ffn, w_router_t):
    b, s, d = x.shape
    tm = min(s, 512)
    n_e = w_router_t.shape[0]
    tok = lambda n: pl.BlockSpec((1, tm, n), lambda bi, i: (bi, i, 0))
    full = lambda shape: pl.BlockSpec(shape, lambda bi, i: tuple(0 for _ in shape))
    return pl.pallas_call(
        _outproj_kernel,
        grid=(b, s // tm),
        in_specs=[tok(o_att.shape[2]), tok(y_ret.shape[2]), tok(d),
                  pl.BlockSpec((1, 6, d), lambda bi, i: (bi, 0, 0)),
                  full(w_o.shape), full((1, d)), full(w_router_t.shape)],
        out_specs=[tok(d), tok(d), pl.BlockSpec((1, n_e, tm), lambda bi, i: (bi, 0, i))],
        out_shape=[jax.ShapeDtypeStruct((b, s, d), F32), jax.ShapeDtypeStruct((b, s, d), F32),
                   jax.ShapeDtypeStruct((b, n_e, s), F32)],
        compiler_params=_cparams(("arbitrary", "arbitrary")),
        name="outproj",
    )(o_att, y_ret, x, mod, w_o, g_ffn, w_router_t)


def _topk_kernel(aff_ref, idx_ref, c_ref, t_ref, *, cap):
    n_e, n_r, _ = c_ref.shape
    assert n_r <= LANE
    a = aff_ref[0]

    def count(mask):
        ones = jnp.where(mask, 1, 0)
        return jnp.sum(jnp.sum(ones, axis=1, keepdims=True), axis=2, keepdims=True)

    def as_float(bits):
        return pltpu.bitcast(bits, F32)

    def search(i, lo):
        cand = lo | jnp.left_shift(jnp.int32(1), 30 - i)
        return jnp.where(count(a >= as_float(cand)) >= cap, cand, lo)

    thr = lax.fori_loop(0, 31, search, jnp.zeros((n_e, 1, 1), jnp.int32))
    above = a >= as_float(thr + 1)
    need = cap - count(above)
    window = (a >= as_float(thr)) & jnp.logical_not(above)
    tok = (lax.broadcasted_iota(jnp.int32, (1, n_r, LANE), 1) * LANE
           + lax.broadcasted_iota(jnp.int32, (1, n_r, LANE), 2))

    def surplus(state):
        return jnp.max(state[1] - need) > 0

    def drop_one(state):
        win, n_win = state
        active = n_win > need
        inside = win > 0.5
        low = jnp.min(jnp.min(jnp.where(inside, a, jnp.inf), axis=1, keepdims=True), axis=2, keepdims=True)
        cand = inside & (a == low)
        last = jnp.max(jnp.max(jnp.where(cand, tok, -1), axis=1, keepdims=True), axis=2, keepdims=True)
        win = jnp.where(cand & (tok == last) & active, 0.0, win)
        return win, n_win - jnp.where(active, 1, 0)

    win, _ = lax.while_loop(surplus, drop_one, (jnp.where(window, 1.0, 0.0), count(window)))
    sel = above | (win > 0.5)

    li = lax.broadcasted_iota(jnp.int32, (LANE, LANE), 0)
    lj = lax.broadcasted_iota(jnp.int32, (LANE, LANE), 1)
    tri = jnp.where(li <= lj, 1.0, 0.0).astype(BF16)
    ones = jnp.ones((LANE, LANE), BF16)
    n_flat = n_e * n_r
    m2 = jnp.where(sel, 1.0, 0.0).astype(BF16).reshape(n_flat, LANE)
    within = _dot(m2, tri)
    tot = _dot(m2, ones)
    pi = lax.broadcasted_iota(jnp.int32, (n_flat, n_flat), 0)
    pj = lax.broadcasted_iota(jnp.int32, (n_flat, n_flat), 1)
    before = jnp.where((pi // n_r == pj // n_r) & (pj < pi), 1.0, 0.0).astype(BF16)
    off = _dot(before, tot.astype(BF16))
    c_ref[...] = (within + off).reshape(n_e, n_r, LANE)
    t_ref[...] = tot.reshape(n_e, n_r, LANE)

    incl = jnp.where(li <= lj, 1.0, 0.0).astype(BF16)[:n_r]
    jcol = lax.broadcasted_iota(jnp.int32, (cap, LANE), 0).astype(F32)
    lane = lax.broadcasted_iota(jnp.int32, (cap, LANE), 1)
    ones8 = jnp.ones((8, LANE), BF16)
    pad = jnp.zeros((LANE - n_r, LANE), F32)

    def per_expert(e, _):
        tot_e = t_ref[e].astype(BF16)
        row_end = _dot_tn(tot_e, incl)[0:1, :]
        row_start = row_end - _dot_tn(tot_e, jnp.where(li == lj, 1.0, 0.0).astype(BF16)[:n_r])[0:1, :]
        full_rows = (row_end <= jcol) & (lane < n_r)
        in_row = (row_start <= jcol) & (row_end > jcol) & (lane < n_r)
        c_pad = jnp.concatenate([c_ref[e], pad], axis=0) if n_r < LANE else c_ref[e]
        c_row = jnp.dot(jnp.where(in_row, 1.0, 0.0), c_pad, preferred_element_type=F32, precision=HIGHEST)
        n_rows = _dot_nt(ones8, jnp.where(full_rows, 1.0, 0.0).astype(BF16))
        n_lanes = _dot_nt(ones8, jnp.where(c_row <= jcol, 1.0, 0.0).astype(BF16))
        idx_ref[0, pl.ds(e, 1), :] = (n_rows[0:1, :] * LANE + n_lanes[0:1, :]).astype(jnp.int32)
        return 0

    lax.fori_loop(0, n_e, per_expert, 0)


def _topk(aff, cap):
    b, n_e, s = aff.shape
    n_r = s // LANE
    aff4 = aff.reshape(b, n_e, n_r, LANE)
    return pl.pallas_call(
        functools.partial(_topk_kernel, cap=cap),
        grid=(b,),
        in_specs=[pl.BlockSpec((1, n_e, n_r, LANE), lambda bi: (bi, 0, 0, 0))],
        out_specs=pl.BlockSpec((1, n_e, cap), lambda bi: (bi, 0, 0)),
        out_shape=jax.ShapeDtypeStruct((b, n_e, cap), jnp.int32),
        scratch_shapes=[pltpu.VMEM((n_e, n_r, LANE), F32), pltpu.VMEM((n_e, n_r, LANE), F32)],
        compiler_params=_cparams(("arbitrary",)),
        name="topk",
    )(aff4)


def _ffn_kernel(idx_ref, aff_ref, h2_hbm, wg_ref, wu_ref, wd_ref, out_hbm,
                h2_v, acc_v, xs_v, y_v, sem, *, tm):
    bi = pl.program_id(0)
    e = pl.program_id(1)
    cap = idx_ref.shape[2]

    @pl.when(e == 0)
    def _():
        cp = pltpu.make_async_copy(h2_hbm.at[bi], h2_v, sem.at[0])
        cp.start()
        acc_v[...] = jnp.zeros_like(acc_v)
        cp.wait()

    def tile(t):
        return pl.ds(pl.multiple_of(t * ROW_TILE, ROW_TILE), ROW_TILE)

    for part in range(cap // tm):
        base = part * tm

        def gather(jb, _):
            for u in range(FFN_UNROLL):
                j = jb * FFN_UNROLL + u
                xs_v[tile(j), :] = h2_v[tile(idx_ref[0, 0, base + j]), :]
            return 0

        lax.fori_loop(0, tm // FFN_UNROLL, gather, 0)
        xb = jnp.concatenate([xs_v[pl.ds(r, tm, stride=ROW_TILE), :] for r in range(ROW_TILE)],
                             axis=1).astype(BF16)
        a = _dot(xb, wg_ref[0])
        u = _dot(xb, wu_ref[0])
        y = _dot((_silu(a) * u).astype(BF16), wd_ref[0])
        for r in range(ROW_TILE):
            y_v[pl.ds(r, tm, stride=ROW_TILE), :] = y[:, r * LANE:(r + 1) * LANE]

        def scatter(jb, _):
            toks = [idx_ref[0, 0, base + jb * FFN_UNROLL + u] for u in range(FFN_UNROLL)]
            rows = [acc_v[tile(t), :] + y_v[tile(jb * FFN_UNROLL + u), :] * aff_ref[0, 0, t]
                    for u, t in enumerate(toks)]
            for t, row in zip(toks, rows):
                acc_v[tile(t), :] = row
            return 0

        lax.fori_loop(0, tm // FFN_UNROLL, scatter, 0)

    @pl.when(e == pl.num_programs(1) - 1)
    def _():
        cp = pltpu.make_async_copy(acc_v, out_hbm.at[bi], sem.at[1])
        cp.start()
        cp.wait()


def _ffn(idx, aff, h2, w_gate, w_up, w_down):
    b, s, d = h2.shape
    n_e, _, f = w_gate.shape
    cap = idx.shape[2]
    tm = min(cap, 256)
    assert d == ROW_TILE * LANE and tm % FFN_UNROLL == 0
    smem = lambda n: pl.BlockSpec((1, 1, n), lambda bi, e: (bi * n_e + e, 0, 0), memory_space=pltpu.SMEM)
    idx = idx.reshape(b * n_e, 1, cap)
    aff = aff.reshape(b * n_e, 1, s)
    rows = s * ROW_TILE
    out = pl.pallas_call(
        functools.partial(_ffn_kernel, tm=tm),
        grid=(b, n_e),
        in_specs=[smem(cap), smem(s),
                  pl.BlockSpec(memory_space=pl.ANY),
                  pl.BlockSpec((1, d, f), lambda bi, e: (e, 0, 0)),
                  pl.BlockSpec((1, d, f), lambda bi, e: (e, 0, 0)),
                  pl.BlockSpec((1, f, d), lambda bi, e: (e, 0, 0))],
        out_specs=pl.BlockSpec(memory_space=pl.ANY),
        out_shape=jax.ShapeDtypeStruct((b, rows, LANE), F32),
        scratch_shapes=[pltpu.VMEM((rows, LANE), F32), pltpu.VMEM((rows, LANE), F32),
                        pltpu.VMEM((tm * ROW_TILE, LANE), F32), pltpu.VMEM((tm * ROW_TILE, LANE), F32),
                        pltpu.SemaphoreType.DMA((2,))],
        compiler_params=_cparams(("arbitrary", "arbitrary"), FFN_VMEM_LIMIT),
        name="ffn",
    )(idx, aff, h2.reshape(b, rows, LANE), w_gate, w_up, w_down)
    return out.reshape(b, s, d)


def _final_kernel(x1_ref, acc_ref, mod_ref, g_ref, o_ref):
    x2 = x1_ref[0] + mod_ref[0, 5:6, :] * acc_ref[0]
    o_ref[0] = _rms(x2) * g_ref[...]


def _final(x1, acc, mod, g_final):
    b, s, d = x1.shape
    tm = min(s, 512)
    tok = pl.BlockSpec((1, tm, d), lambda bi, i: (bi, i, 0))
    return pl.pallas_call(
        _final_kernel,
        grid=(b, s // tm),
        in_specs=[tok, tok, pl.BlockSpec((1, 6, d), lambda bi, i: (bi, 0, 0)),
                  pl.BlockSpec((1, d), lambda bi, i: (0, 0))],
        out_specs=tok,
        out_shape=jax.ShapeDtypeStruct((b, s, d), F32),
        compiler_params=_cparams(("arbitrary", "arbitrary")),
        name="final",
    )(x1, acc, mod, g_final.reshape(1, d))


def kernel(x, c, ctx, c_ctx, w_ada, b_ada, g_norm_mix, g_norm_ffn, w_in, g_q_lora, g_kv_lora, w_uq, w_ukv,
           ret_exp_fwd, ret_exp_bwd, g_ret, w_o, w_router, w_exp_gate, w_exp_up, w_exp_down, g_final):
    b, s, d = x.shape
    depth = w_ada.shape[0]
    assert depth == 1, "the context stream update is only needed for depth > 1"
    assert s % LANE == 0 and s % GRID_W == 0
    cap = EC_CAPACITY * s // N_EXPERTS

    rows = -(-(b + 1) // 8) * 8
    cc = jnp.zeros((rows, d), F32).at[:b].set(c).at[b].set(c_ctx)
    mods = _ada(cc, w_ada[0], b_ada[0])
    mod = mods[:b].reshape(b, 6, d)
    mod_c = mods[b:b + 1].reshape(1, 6, d)

    w_in_ext, w_uq_ext, w_ukv_ext = _prep_weights(w_in[0], w_uq[0], w_ukv[0])
    tabs = _rope_tables(s)
    gmix = g_norm_mix[0].reshape(1, d)
    gq = g_q_lora[0].reshape(1, -1)
    gkv = g_kv_lora[0].reshape(1, -1)

    qa, ka, va, rq, rk, rv, rg = _inproj(x, mod, gmix, w_in_ext, gq, gkv, w_uq_ext, w_ukv_ext, tabs, is_ctx=False)
    ka_c, va_c, rk_c, rv_c = _inproj(ctx, mod_c, gmix, w_in_ext, gq, gkv, w_uq_ext, w_ukv_ext, None, is_ctx=True)

    o_att = _attention(qa, ka, va, ka_c, va_c)
    y_ret = _retention(rq, rk, rv, rg, rk_c, rv_c, ret_exp_fwd[0], ret_exp_bwd[0], g_ret[0])

    x1, h2, aff = _outproj(o_att, y_ret, x, mod, w_o[0].astype(BF16), g_norm_ffn[0].reshape(1, d),
                           w_router[0].T)
    idx = _topk(aff, cap)
    acc = _ffn(idx, aff, h2, w_exp_gate[0].astype(BF16), w_exp_up[0].astype(BF16), w_exp_down[0].astype(BF16))
    return _final(x1, acc, mod, g_final)
```

```python
import functools

import numpy as np
import jax
import jax.numpy as jnp
from jax import lax
from jax.experimental import pallas as pl
from jax.experimental.pallas import tpu as pltpu

GRID_W = 64
MLA_HEADS = 8
MLA_Q_RANK = 256
MLA_KV_RANK = 128
MLA_NOPE = 64
MLA_ROPE = 32
MLA_V = 64
RET_HEADS = 8
RET_DK = 64
RET_DV = 64
RET_CHUNK = 128
N_EXPERTS = 16
EC_CAPACITY = 2
ROPE_BASE = 10000.0
EPS = 1e-6

LANE = 128
HEAD_TILE = 128
ATTN_TQ = 256
ATTN_KEY_BLOCK = 64
RET_UNROLL = 4
ROW_TILE = 8
FFN_UNROLL = 8
VMEM_LIMIT = 48 * 1024 * 1024
FFN_VMEM_LIMIT = 60 * 1024 * 1024
LOG2E = 1.4426950408889634

F32 = jnp.float32
BF16 = jnp.bfloat16
HIGHEST = lax.Precision.HIGHEST

W_CQ = (0, 256)
W_CKV = (256, 384)
W_KR = (384, 512)
W_RQ = (512, 1024)
W_RK = (1024, 1536)
W_RV = (1536, 2048)
W_RG = (2048, 2560)
D_IN_EXT = 2560


def _cparams(sem, limit=VMEM_LIMIT):
    return pltpu.CompilerParams(dimension_semantics=sem, vmem_limit_bytes=limit)


def _dot(a, b):
    return jnp.dot(a, b, preferred_element_type=F32)


def _dot_nt(a, b):
    return lax.dot_general(a, b, (((1,), (1,)), ((), ())), preferred_element_type=F32)


def _dot_tn(a, b):
    return lax.dot_general(a, b, (((0,), (0,)), ((), ())), preferred_element_type=F32)


def _silu(x):
    return x * jax.nn.sigmoid(x)


def _mla_lane_maps():
    half = MLA_ROPE // 2
    src = np.zeros(HEAD_TILE, np.int32)
    valid = np.zeros(HEAD_TILE, bool)
    src[0:half] = MLA_NOPE + np.arange(half)
    valid[0:half] = True
    src[half:64] = np.arange(64 - half)
    valid[half:64] = True
    src[64:64 + half] = MLA_NOPE + half + np.arange(half)
    valid[64:64 + half] = True
    n_rest = MLA_NOPE - (64 - half)
    src[64 + half:64 + half + n_rest] = (64 - half) + np.arange(n_rest)
    valid[64 + half:64 + half + n_rest] = True
    return src, valid


def _prep_weights(w_in, w_uq, w_ukv):
    d_model = w_in.shape[0]
    half = MLA_ROPE // 2
    src, valid = _mla_lane_maps()
    is_rope = valid & (src >= MLA_NOPE)
    is_nope = valid & (src < MLA_NOPE)

    o = 0
    cq = w_in[:, o:o + MLA_Q_RANK]; o += MLA_Q_RANK
    ckv = w_in[:, o:o + MLA_KV_RANK]; o += MLA_KV_RANK
    kr = w_in[:, o:o + MLA_ROPE]; o += MLA_ROPE
    n_r = RET_HEADS * RET_DK
    rq = w_in[:, o:o + n_r]; o += n_r
    rk = w_in[:, o:o + n_r]; o += n_r
    rv = w_in[:, o:o + RET_HEADS * RET_DV]; o += RET_HEADS * RET_DV
    rg = w_in[:, o:o + RET_HEADS * RET_DV]

    kr_idx = np.where(is_rope, src - MLA_NOPE, 0)
    kr_ext = jnp.where(jnp.asarray(is_rope)[None, :], kr[:, kr_idx], 0.0)

    hd = RET_DK // 2
    lanes = np.arange(n_r)
    tile, l = lanes // LANE, lanes % LANE
    grp = l // hd
    head = 2 * tile + (grp % 2)
    perm = head * RET_DK + (grp // 2) * hd + (l % hd)
    rq_p = rq[:, perm]
    rk_p = rk[:, perm] * (RET_DK ** -0.5)

    w_in_ext = jnp.concatenate([cq, ckv, kr_ext, rq_p, rk_p, rv, rg], axis=1).astype(BF16)
    assert w_in_ext.shape == (d_model, D_IN_EXT)

    dq = MLA_NOPE + MLA_ROPE
    q_cols = (np.arange(MLA_HEADS)[:, None] * dq + src[None, :]).reshape(-1)
    q_valid = np.tile(valid, MLA_HEADS)
    w_uq_ext = jnp.where(jnp.asarray(q_valid)[None, :], w_uq[:, q_cols], 0.0).astype(BF16)

    dkv = MLA_NOPE + MLA_V
    k_cols = (np.arange(MLA_HEADS)[:, None] * dkv + np.where(is_nope, src, 0)[None, :]).reshape(-1)
    k_valid = np.tile(is_nope, MLA_HEADS)
    w_uk_ext = jnp.where(jnp.asarray(k_valid)[None, :], w_ukv[:, k_cols], 0.0)
    v_cols = (np.arange(MLA_HEADS)[:, None] * dkv + MLA_NOPE + np.arange(MLA_V)[None, :]).reshape(-1)
    w_uv = w_ukv[:, v_cols]
    w_ukv_ext = jnp.concatenate([w_uk_ext, w_uv], axis=1).astype(BF16)
    return w_in_ext, w_uq_ext, w_ukv_ext


def _rope_patterns():
    pat = np.zeros((8, LANE), np.float32)
    half = MLA_ROPE // 2
    nf = MLA_ROPE // 4
    inv = ROPE_BASE ** (-np.arange(nf, dtype=np.float64) / nf)
    for base, sign in ((0, -1.0), (64, 1.0)):
        for i in range(half):
            pat[0, base + i] = inv[i % nf]
            pat[1, base + i] = 1.0 if i < nf else 0.0
            pat[2, base + i] = sign
    hd = RET_DK // 2
    nf = RET_DK // 4
    inv = ROPE_BASE ** (-np.arange(nf, dtype=np.float64) / nf)
    for l in range(LANE):
        i = l % hd
        pat[3, l] = inv[i % nf]
        pat[4, l] = 1.0 if i < nf else 0.0
        pat[5, l] = -1.0 if l < 64 else 1.0
    return jnp.asarray(pat)


def _ada_kernel(c_ref, w_ref, b_ref, o_ref):
    s = _silu(c_ref[...])
    o_ref[...] = jnp.dot(s, w_ref[...], preferred_element_type=F32, precision=HIGHEST) + b_ref[...]


def _ada(cc, w_ada, b_ada):
    rows, d = cc.shape
    n = w_ada.shape[1]
    tn = 1024
    return pl.pallas_call(
        _ada_kernel,
        grid=(n // tn,),
        in_specs=[pl.BlockSpec((rows, d), lambda j: (0, 0)),
                  pl.BlockSpec((d, tn), lambda j: (0, j)),
                  pl.BlockSpec((1, tn), lambda j: (0, j))],
        out_specs=pl.BlockSpec((rows, tn), lambda j: (0, j)),
        out_shape=jax.ShapeDtypeStruct((rows, n), F32),
        compiler_params=_cparams(("arbitrary",)),
        name="ada",
    )(cc, w_ada, b_ada.reshape(1, n))


def _rope_kernel(pat_ref, o_ref, *, tm):
    t = pl.program_id(0) * tm + lax.broadcasted_iota(jnp.int32, (tm, LANE), 0)
    row = (t // GRID_W).astype(F32)
    col = (t % GRID_W).astype(F32)
    for k in range(2):
        inv = pat_ref[3 * k:3 * k + 1, :]
        use_row = pat_ref[3 * k + 1:3 * k + 2, :]
        sign = pat_ref[3 * k + 2:3 * k + 3, :]
        ang = jnp.where(use_row > 0.5, row, col) * inv
        active = sign != 0.0
        o_ref[2 * k] = jnp.where(active, jnp.cos(ang), 1.0)
        o_ref[2 * k + 1] = jnp.where(active, sign * jnp.sin(ang), 0.0)


def _rope_tables(seq):
    tm = min(seq, 512)
    return pl.pallas_call(
        functools.partial(_rope_kernel, tm=tm),
        grid=(seq // tm,),
        in_specs=[pl.BlockSpec((8, LANE), lambda i: (0, 0))],
        out_specs=pl.BlockSpec((4, tm, LANE), lambda i: (0, i, 0)),
        out_shape=jax.ShapeDtypeStruct((4, seq, LANE), F32),
        compiler_params=_cparams(("arbitrary",)),
        name="rope",
    )(_rope_patterns())


def _rms(x):
    return x * lax.rsqrt(jnp.mean(x * x, axis=-1, keepdims=True) + EPS)


def _rot(x, cos, sin):
    return x * cos + pltpu.roll(x, 64, axis=1) * sin


def _inproj_kernel(*refs, is_ctx):
    if is_ctx:
        (x_ref, mod_ref, gmix_ref, win_ref, gkv_ref, wukv_ref,
         ka_ref, va_ref, rk_ref, rv_ref) = refs
    else:
        (x_ref, mod_ref, gmix_ref, win_ref, gq_ref, gkv_ref, wuq_ref, wukv_ref, tab_ref,
         qa_ref, ka_ref, va_ref, rq_ref, rk_ref, rv_ref, rg_ref) = refs
    n_k = MLA_HEADS * HEAD_TILE
    h = _rms(x_ref[0]) * gmix_ref[...] * (1.0 + mod_ref[0, 1:2, :]) + mod_ref[0, 0:1, :]
    z = _dot(h.astype(BF16), win_ref[...])
    ckv = _rms(z[:, W_CKV[0]:W_CKV[1]]) * gkv_ref[...]
    kv = _dot(ckv.astype(BF16), wukv_ref[...])
    kr = z[:, W_KR[0]:W_KR[1]]
    if not is_ctx:
        cm, sm, cr, sr = tab_ref[0], tab_ref[1], tab_ref[2], tab_ref[3]
        kr = _rot(kr, cm, sm)
    for hh in range(MLA_HEADS):
        ka_ref[0, :, hh * HEAD_TILE:(hh + 1) * HEAD_TILE] = (
            kv[:, hh * HEAD_TILE:(hh + 1) * HEAD_TILE] + kr).astype(BF16)
    for hp in range(MLA_HEADS // 2):
        va_ref[0, hp, 0] = kv[:, n_k + hp * LANE:n_k + (hp + 1) * LANE].T.astype(BF16)
    rv_ref[0] = z[:, W_RV[0]:W_RV[1]].astype(BF16)
    n_t = (W_RK[1] - W_RK[0]) // LANE
    if is_ctx:
        rk_ref[0] = z[:, W_RK[0]:W_RK[1]].astype(BF16)
        return
    for t in range(n_t):
        rk_ref[0, :, t * LANE:(t + 1) * LANE] = _rot(
            z[:, W_RK[0] + t * LANE:W_RK[0] + (t + 1) * LANE], cr, sr).astype(BF16)
        rq_ref[0, :, t * LANE:(t + 1) * LANE] = _rot(
            z[:, W_RQ[0] + t * LANE:W_RQ[0] + (t + 1) * LANE], cr, sr).astype(BF16)
    rg_ref[0] = z[:, W_RG[0]:W_RG[1]].astype(BF16)
    cq = _rms(z[:, W_CQ[0]:W_CQ[1]]) * gq_ref[...]
    q = _dot(cq.astype(BF16), wuq_ref[...])
    qscale = (MLA_NOPE + MLA_ROPE) ** -0.5 * LOG2E
    for hh in range(MLA_HEADS):
        qh = _rot(q[:, hh * HEAD_TILE:(hh + 1) * HEAD_TILE], cm, sm)
        qa_ref[0, :, hh * HEAD_TILE:(hh + 1) * HEAD_TILE] = (qh * qscale).astype(BF16)


def _inproj(x, mod, gmix, w_in_ext, gq, gkv, w_uq_ext, w_ukv_ext, tabs, *, is_ctx):
    b, s, d = x.shape
    tm = min(s, 512)
    n_k = MLA_HEADS * HEAD_TILE
    n_r = RET_HEADS * RET_DK
    hp = MLA_HEADS // 2
    per_b = mod.shape[0] > 1
    full = lambda shape: pl.BlockSpec(shape, lambda bi, i: tuple(0 for _ in shape))
    tok = lambda n: pl.BlockSpec((1, tm, n), lambda bi, i: (bi, i, 0))
    tok_shape = lambda n: jax.ShapeDtypeStruct((b, s, n), BF16)
    mod_spec = pl.BlockSpec((1, 6, d), (lambda bi, i: (bi, 0, 0)) if per_b else (lambda bi, i: (0, 0, 0)))
    vt_spec = pl.BlockSpec((1, hp, 1, LANE, tm), lambda bi, i: (bi, 0, i, 0, 0))
    vt_shape = jax.ShapeDtypeStruct((b, hp, s // tm, LANE, tm), BF16)
    if is_ctx:
        ins = [x, mod, gmix, w_in_ext, gkv, w_ukv_ext]
        in_specs = [tok(d), mod_spec, full((1, d)), full(w_in_ext.shape), full((1, MLA_KV_RANK)),
                    full(w_ukv_ext.shape)]
        outs = [(tok(n_k), tok_shape(n_k)), (vt_spec, vt_shape)] + [(tok(n_r), tok_shape(n_r))] * 2
    else:
        ins = [x, mod, gmix, w_in_ext, gq, gkv, w_uq_ext, w_ukv_ext, tabs]
        in_specs = [tok(d), mod_spec, full((1, d)), full(w_in_ext.shape), full((1, MLA_Q_RANK)),
                    full((1, MLA_KV_RANK)), full(w_uq_ext.shape), full(w_ukv_ext.shape),
                    pl.BlockSpec((4, tm, LANE), lambda bi, i: (0, i, 0))]
        outs = [(tok(n_k), tok_shape(n_k))] * 2 + [(vt_spec, vt_shape)] + [(tok(n_r), tok_shape(n_r))] * 4
    return pl.pallas_call(
        functools.partial(_inproj_kernel, is_ctx=is_ctx),
        grid=(b, s // tm),
        in_specs=in_specs,
        out_specs=[o[0] for o in outs],
        out_shape=[o[1] for o in outs],
        compiler_params=_cparams(("arbitrary", "arbitrary")),
        name="inproj_ctx" if is_ctx else "inproj",
    )(*ins)


def _attn_kernel(q_ref, kl_ref, kc_ref, vl_ref, vc_ref, o_ref, sa_ref, sb_ref, *, tk):
    tq = q_ref.shape[1]
    n_chunks = kl_ref.shape[1] // tk
    lc = kc_ref.shape[1]
    assert n_chunks >= 2 and n_chunks % 2 == 0 and lc <= tk
    heads = tuple((hh * HEAD_TILE, (hh + 1) * HEAD_TILE) for hh in range(2))

    def scores_into(s_ref, k2):
        maxes = []
        for hh, (lo, hi) in enumerate(heads):
            s = _dot_nt(k2[:, lo:hi], q_ref[0, :, lo:hi])
            s_ref[hh, 0:k2.shape[0], :] = s
            maxes.append(jnp.max(s, axis=0, keepdims=True))
        return tuple(maxes)

    def consume(carries, s_ref, n_keys, maxes, vt):
        out = []
        for hh in range(2):
            m, l, acc = carries[hh]
            m_new = jnp.maximum(m, maxes[hh])
            alpha = jnp.exp2(m - m_new)
            blocks, l_add = [], jnp.zeros_like(l)
            for r in range(0, n_keys, ATTN_KEY_BLOCK):
                p = jnp.exp2(s_ref[hh, r:r + ATTN_KEY_BLOCK, :] - m_new)
                l_add = l_add + jnp.sum(p, axis=0, keepdims=True)
                blocks.append(p.astype(BF16))
            acc = alpha * acc + _dot(vt, jnp.concatenate(blocks, axis=0))
            out.append((m_new, alpha * l + l_add, acc))
        return tuple(out)

    def keys(c):
        return kl_ref[0, pl.ds(pl.multiple_of(c * tk, tk), tk), :]

    def body(i, state):
        carries, max_a = state
        c = 2 * i
        max_b = scores_into(sb_ref, keys(c + 1))
        carries = consume(carries, sa_ref, tk, max_a, vl_ref[0, 0, c])
        max_a = scores_into(sa_ref, keys(c + 2))
        carries = consume(carries, sb_ref, tk, max_b, vl_ref[0, 0, c + 1])
        return carries, max_a

    init = (jnp.full((1, tq), -jnp.inf, F32), jnp.zeros((1, tq), F32), jnp.zeros((LANE, tq), F32))
    state = ((init, init), scores_into(sa_ref, kl_ref[0, 0:tk, :]))
    carries, max_a = lax.fori_loop(0, (n_chunks - 2) // 2, body, state)
    max_b = scores_into(sb_ref, kl_ref[0, (n_chunks - 1) * tk:n_chunks * tk, :])
    carries = consume(carries, sa_ref, tk, max_a, vl_ref[0, 0, n_chunks - 2])
    max_c = scores_into(sa_ref, kc_ref[0])
    carries = consume(carries, sb_ref, tk, max_b, vl_ref[0, 0, n_chunks - 1])
    carries = consume(carries, sa_ref, lc, max_c, vc_ref[0, 0, 0])
    (_, l_a, acc_a), (_, l_b, acc_b) = carries
    out_t = jnp.concatenate([(acc_a / l_a)[:MLA_V], (acc_b / l_b)[MLA_V:]], axis=0)
    o_ref[0] = out_t.T.astype(BF16)


def _attention(qa, ka, vt, ka_c, vt_c):
    b, s, _ = qa.shape
    lc = ka_c.shape[1]
    hp, n_chunks, _, tk = vt.shape[1:]
    tq = min(s, ATTN_TQ)
    return pl.pallas_call(
        functools.partial(_attn_kernel, tk=tk),
        grid=(b, hp, s // tq),
        in_specs=[pl.BlockSpec((1, tq, 2 * HEAD_TILE), lambda bi, h, i: (bi, i, h)),
                  pl.BlockSpec((1, s, 2 * HEAD_TILE), lambda bi, h, i: (bi, 0, h)),
                  pl.BlockSpec((1, lc, 2 * HEAD_TILE), lambda bi, h, i: (bi, 0, h)),
                  pl.BlockSpec((1, 1, n_chunks, LANE, tk), lambda bi, h, i: (bi, h, 0, 0, 0)),
                  pl.BlockSpec((1, 1, 1, LANE, lc), lambda bi, h, i: (bi, h, 0, 0, 0))],
        out_specs=pl.BlockSpec((1, tq, 2 * MLA_V), lambda bi, h, i: (bi, i, h)),
        out_shape=jax.ShapeDtypeStruct((b, s, MLA_HEADS * MLA_V), BF16),
        scratch_shapes=[pltpu.VMEM((2, tk, tq), F32), pltpu.VMEM((2, tk, tq), F32)],
        compiler_params=_cparams(("arbitrary", "arbitrary", "arbitrary")),
        name="attn",
    )(qa, ka, ka_c, vt, vt_c)


def _ret_kernel(q_ref, k_ref, v_ref, g_ref, kc_ref, vc_ref, eh_ref, el_ref, gret_ref, o_ref, rb_ref):
    c = RET_CHUNK
    n_lat = q_ref.shape[1] // c
    n_ctx = kc_ref.shape[1] // c

    def log_gamma(e):
        return jnp.log1p(-jnp.exp2(-e))

    lgf_a, lgf_b = log_gamma(eh_ref[0, 0:1, :]), log_gamma(eh_ref[0, 1:2, :])
    lgb_a, lgb_b = log_gamma(eh_ref[0, 2:3, :]), log_gamma(eh_ref[0, 3:4, :])
    lgf_l, lgb_l = log_gamma(el_ref[0, 0:1, :]), log_gamma(el_ref[0, 1:2, :])

    ri = lax.broadcasted_iota(jnp.int32, (c, c), 0)
    ci = lax.broadcasted_iota(jnp.int32, (c, c), 1)
    diff = (ri - ci).astype(F32)

    def decay(lgf, lgb):
        return jnp.where(diff >= 0, jnp.exp(lgf * diff), jnp.exp(-lgb * diff))

    d_a, d_b = decay(lgf_a, lgb_a), decay(lgf_b, lgb_b)
    pos = lax.broadcasted_iota(jnp.int32, (c, LANE), 0).astype(F32)
    xi_f = jnp.exp(lgf_l * (pos + 1.0))
    xi_b = jnp.exp(lgb_l * (c - pos))
    zeta_f = jnp.exp(lgf_l * (c - 1.0 - pos))
    zeta_b = jnp.exp(lgb_l * pos)
    hd = RET_DK // 2
    row_is_a = (ri // hd) % 2 == 0
    col_is_a = ci < RET_DV
    gc_f = jnp.exp(jnp.where(row_is_a, lgf_a, lgf_b) * c)
    gc_b = jnp.exp(jnp.where(row_is_a, lgb_a, lgb_b) * c)
    same_head = (row_is_a == col_is_a).astype(F32)
    lane = lax.broadcasted_iota(jnp.int32, (c, LANE), 1)
    qk_is_a = (lane // hd) % 2 == 0
    v_is_a = lane < RET_DV

    def state_update(r, gc, k, v, zeta):
        return gc * r + _dot_tn((k.astype(F32) * zeta).astype(BF16), v)

    def lat(ref, n):
        return ref[0, pl.ds(pl.multiple_of(n * c, c), c), :]

    zero = jnp.zeros((c, LANE), F32)

    r = zero
    for cc in reversed(range(n_ctx)):
        r = state_update(r, gc_b, kc_ref[0, cc * c:(cc + 1) * c, :], vc_ref[0, cc * c:(cc + 1) * c, :], zeta_b)

    def key_value_sum(k, v, zeta):
        return _dot_tn((k.astype(F32) * zeta).astype(BF16), v)

    def bwd_body(i, r):
        ns = [n_lat - 1 - (i * RET_UNROLL + u) for u in range(RET_UNROLL)]
        sums = [key_value_sum(lat(k_ref, n), lat(v_ref, n), zeta_b) for n in ns]
        for n, kv in zip(ns, sums):
            rb_ref[n] = r
            r = gc_b * r + kv
        return r

    assert n_lat % RET_UNROLL == 0
    lax.fori_loop(0, n_lat // RET_UNROLL, bwd_body, r)

    r = zero
    for cc in range(n_ctx):
        r = state_update(r, gc_f, kc_ref[0, cc * c:(cc + 1) * c, :], vc_ref[0, cc * c:(cc + 1) * c, :], zeta_f)

    gret = gret_ref[...]

    def fwd_body(i, r):
        ns = [i * RET_UNROLL + u for u in range(RET_UNROLL)]
        sums = [key_value_sum(lat(k_ref, n), lat(v_ref, n), zeta_f) for n in ns]
        for n, kv in zip(ns, sums):
            chunk_out(n, r)
            r = gc_f * r + kv
        return r

    def chunk_out(n, r):
        q, k, v = lat(q_ref, n), lat(k_ref, n), lat(v_ref, n)
        qz = jnp.zeros_like(q)
        s_a = _dot_nt(jnp.where(qk_is_a, q, qz), k) * d_a
        s_b = _dot_nt(jnp.where(qk_is_a, qz, q), k) * d_b
        vz = jnp.zeros_like(v)
        p2 = jnp.concatenate([s_a, s_b], axis=1).astype(BF16)
        v2 = jnp.concatenate([jnp.where(v_is_a, v, vz), jnp.where(v_is_a, vz, v)], axis=0)
        qf = q.astype(F32)
        q2 = jnp.concatenate([(qf * xi_f).astype(BF16), (qf * xi_b).astype(BF16)], axis=1)
        r2 = jnp.concatenate([(r * same_head).astype(BF16), (rb_ref[n] * same_head).astype(BF16)], axis=0)
        y = _dot(p2, v2) + _dot(q2, r2)
        inv_n = 1.0 / RET_DV
        sum_a = jnp.sum(jnp.where(v_is_a, y, 0.0), axis=-1, keepdims=True)
        sum_b = jnp.sum(jnp.where(v_is_a, 0.0, y), axis=-1, keepdims=True)
        dlt = y - jnp.where(v_is_a, sum_a, sum_b) * inv_n
        sq = dlt * dlt
        var_a = jnp.sum(jnp.where(v_is_a, sq, 0.0), axis=-1, keepdims=True)
        var_b = jnp.sum(jnp.where(v_is_a, 0.0, sq), axis=-1, keepdims=True)
        yn = dlt * lax.rsqrt(jnp.where(v_is_a, var_a, var_b) * inv_n + EPS) * gret
        o_ref[0, pl.ds(pl.multiple_of(n * c, c), c), :] = (yn * _silu(lat(g_ref, n).astype(F32))).astype(BF16)

    lax.fori_loop(0, n_lat // RET_UNROLL, fwd_body, r)


def _retention(rq, rk, rv, rg, rk_c, rv_c, exp_f, exp_b, g_ret):
    b, s, n = rq.shape
    lc = rk_c.shape[1]
    n_t = n // LANE
    ef = exp_f.reshape(n_t, 2)
    eb = exp_b.reshape(n_t, 2)
    eh = jnp.broadcast_to(jnp.concatenate([ef, eb], axis=1)[:, :, None], (n_t, 4, LANE)).astype(F32)
    lane_is_b = (np.arange(LANE) // (RET_DK // 2)) % 2
    el = jnp.stack([ef[:, lane_is_b], eb[:, lane_is_b]], axis=1).astype(F32)
    tok = lambda length: pl.BlockSpec((1, length, LANE), lambda bi, t: (bi, 0, t))
    return pl.pallas_call(
        _ret_kernel,
        grid=(b, n_t),
        in_specs=[tok(s), tok(s), tok(s), tok(s), tok(lc), tok(lc),
                  pl.BlockSpec((1, 4, LANE), lambda bi, t: (t, 0, 0)),
                  pl.BlockSpec((1, 2, LANE), lambda bi, t: (t, 0, 0)),
                  pl.BlockSpec((1, LANE), lambda bi, t: (0, t))],
        out_specs=tok(s),
        out_shape=jax.ShapeDtypeStruct((b, s, n), BF16),
        scratch_shapes=[pltpu.VMEM((s // RET_CHUNK, RET_CHUNK, LANE), F32)],
        compiler_params=_cparams(("arbitrary", "arbitrary")),
        name="ret",
    )(rq, rk, rv, rg, rk_c, rv_c, eh, el, g_ret.reshape(1, n))


def _outproj_kernel(oa_ref, yr_ref, x_ref, mod_ref, wo_ref, gffn_ref, wr_ref, x1_ref, h2_ref, aff_ref):
    n_a = oa_ref.shape[2]
    y = _dot(oa_ref[0], wo_ref[0:n_a, :]) + _dot(yr_ref[0], wo_ref[n_a:, :])
    x1 = x_ref[0] + mod_ref[0, 2:3, :] * y
    x1_ref[0] = x1
    h2 = _rms(x1) * gffn_ref[...] * (1.0 + mod_ref[0, 4:5, :]) + mod_ref[0, 3:4, :]
    tm = h2.shape[0]
    for r in range(ROW_TILE):
        h2_ref[0, pl.ds(r, tm, stride=ROW_TILE), :] = h2[:, r * LANE:(r + 1) * LANE]
    logits =lax.dot_general(wr_ref[...], h2, (((1,), (1,)), ((), ())),
                             preferred_element_type=F32, precision=HIGHEST)
    ex = jnp.exp(logits - jnp.max(logits, axis=0, keepdims=True))
    aff_ref[0] = ex / jnp.sum(ex, axis=0, keepdims=True)


def _outproj(o_att, y_ret, x, mod, w_o, g_ffn, w_router_t):
    b, s, d = x.shape
    tm = min(s, 512)
    n_e = w_router_t.shape[0]
    tok = lambda n: pl.BlockSpec((1, tm, n), lambda bi, i: (bi, i, 0))
    full = lambda shape: pl.BlockSpec(shape, lambda bi, i: tuple(0 for _ in shape))
    return pl.pallas_call(
        _outproj_kernel,
        grid=(b, s // tm),
        in_specs=[tok(o_att.shape[2]), tok(y_ret.shape[2]), tok(d),
                  pl.BlockSpec((1, 6, d), lambda bi, i: (bi, 0, 0)),
                  full(w_o.shape), full((1, d)), full(w_router_t.shape)],
        out_specs=[tok(d), pl.BlockSpec((1, tm * ROW_TILE, LANE), lambda bi, i: (bi, i, 0)),
                   pl.BlockSpec((1, n_e, tm), lambda bi, i: (bi, 0, i))],
        out_shape=[jax.ShapeDtypeStruct((b, s, d), F32), jax.ShapeDtypeStruct((b, s * ROW_TILE, LANE), F32),
                   jax.ShapeDtypeStruct((b, n_e, s), F32)],
        compiler_params=_cparams(("arbitrary", "arbitrary")),
        name="outproj",
    )(o_att, y_ret, x, mod, w_o, g_ffn, w_router_t)


def _topk_kernel(aff_ref, idx_ref, c_ref, t_ref, *, cap):
    n_e, n_r, _ = c_ref.shape
    assert n_r <= LANE
    a = aff_ref[0]

    def count(mask):
        ones = jnp.where(mask, 1, 0)
        return jnp.sum(jnp.sum(ones, axis=1, keepdims=True), axis=2, keepdims=True)

    def as_float(bits):
        return pltpu.bitcast(bits, F32)

    def search(i, lo):
        cand = lo | jnp.left_shift(jnp.int32(1), 30 - i)
        return jnp.where(count(a >= as_float(cand)) >= cap, cand, lo)

    thr = lax.fori_loop(0, 31, search, jnp.zeros((n_e, 1, 1), jnp.int32))
    above = a >= as_float(thr + 1)
    need = cap - count(above)
    window = (a >= as_float(thr)) & jnp.logical_not(above)
    tok = (lax.broadcasted_iota(jnp.int32, (1, n_r, LANE), 1) * LANE
           + lax.broadcasted_iota(jnp.int32, (1, n_r, LANE), 2))

    def surplus(state):
        return jnp.max(state[1] - need) > 0

    def drop_one(state):
        win, n_win = state
        active = n_win > need
        inside = win > 0.5
        low = jnp.min(jnp.min(jnp.where(inside, a, jnp.inf), axis=1, keepdims=True), axis=2, keepdims=True)
        cand = inside & (a == low)
        last = jnp.max(jnp.max(jnp.where(cand, tok, -1), axis=1, keepdims=True), axis=2, keepdims=True)
        win = jnp.where(cand & (tok == last) & active, 0.0, win)
        return win, n_win - jnp.where(active, 1, 0)

    win, _ = lax.while_loop(surplus, drop_one, (jnp.where(window, 1.0, 0.0), count(window)))
    sel = above | (win > 0.5)

    li = lax.broadcasted_iota(jnp.int32, (LANE, LANE), 0)
    lj = lax.broadcasted_iota(jnp.int32, (LANE, LANE), 1)
    tri = jnp.where(li <= lj, 1.0, 0.0).astype(BF16)
    ones = jnp.ones((LANE, LANE), BF16)
    n_flat = n_e * n_r
    m2 = jnp.where(sel, 1.0, 0.0).astype(BF16).reshape(n_flat, LANE)
    within = _dot(m2, tri)
    tot = _dot(m2, ones)
    pi = lax.broadcasted_iota(jnp.int32, (n_flat, n_flat), 0)
    pj = lax.broadcasted_iota(jnp.int32, (n_flat, n_flat), 1)
    before = jnp.where((pi // n_r == pj // n_r) & (pj < pi), 1.0, 0.0).astype(BF16)
    off = _dot(before, tot.astype(BF16))
    c_ref[...] = (within + off).reshape(n_e, n_r, LANE)
    t_ref[...] = tot.reshape(n_e, n_r, LANE)

    incl = jnp.where(li <= lj, 1.0, 0.0).astype(BF16)[:n_r]
    jcol = lax.broadcasted_iota(jnp.int32, (cap, LANE), 0).astype(F32)
    lane = lax.broadcasted_iota(jnp.int32, (cap, LANE), 1)
    ones8 = jnp.ones((8, LANE), BF16)
    pad = jnp.zeros((LANE - n_r, LANE), F32)

    def per_expert(e, _):
        tot_e = t_ref[e].astype(BF16)
        row_end = _dot_tn(tot_e, incl)[0:1, :]
        row_start = row_end - _dot_tn(tot_e, jnp.where(li == lj, 1.0, 0.0).astype(BF16)[:n_r])[0:1, :]
        full_rows = (row_end <= jcol) & (lane < n_r)
        in_row = (row_start <= jcol) & (row_end > jcol) & (lane < n_r)
        c_pad = jnp.concatenate([c_ref[e], pad], axis=0) if n_r < LANE else c_ref[e]
        c_row = jnp.dot(jnp.where(in_row, 1.0, 0.0), c_pad, preferred_element_type=F32, precision=HIGHEST)
        n_rows = _dot_nt(ones8, jnp.where(full_rows, 1.0, 0.0).astype(BF16))
        n_lanes = _dot_nt(ones8, jnp.where(c_row <= jcol, 1.0, 0.0).astype(BF16))
        idx_ref[0, pl.ds(e, 1), :] = (n_rows[0:1, :] * LANE + n_lanes[0:1, :]).astype(jnp.int32)
        return 0

    lax.fori_loop(0, n_e, per_expert, 0)


def _topk(aff, cap):
    b, n_e, s = aff.shape
    n_r = s // LANE
    aff4 = aff.reshape(b, n_e, n_r, LANE)
    return pl.pallas_call(
        functools.partial(_topk_kernel, cap=cap),
        grid=(b,),
        in_specs=[pl.BlockSpec((1, n_e, n_r, LANE), lambda bi: (bi, 0, 0, 0))],
        out_specs=pl.BlockSpec((1, n_e, cap), lambda bi: (bi, 0, 0)),
        out_shape=jax.ShapeDtypeStruct((b, n_e, cap), jnp.int32),
        scratch_shapes=[pltpu.VMEM((n_e, n_r, LANE), F32), pltpu.VMEM((n_e, n_r, LANE), F32)],
        compiler_params=_cparams(("arbitrary",)),
        name="topk",
    )(aff4)


def _ffn_kernel(idx_ref, aff_ref, h2_hbm, wg_ref, wu_ref, wd_ref, out_hbm,
                h2_v, acc_v, xs_v, y_v, sem, *, tm):
    bi = pl.program_id(0)
    e = pl.program_id(1)
    cap = idx_ref.shape[2]

    @pl.when(e == 0)
    def _():
        cp = pltpu.make_async_copy(h2_hbm.at[bi], h2_v, sem.at[0])
        cp.start()
        acc_v[...] = jnp.zeros_like(acc_v)
        cp.wait()

    def tile(t):
        return pl.ds(pl.multiple_of(t * ROW_TILE, ROW_TILE), ROW_TILE)

    for part in range(cap // tm):
        base = part * tm

        def gather(jb, _):
            for u in range(FFN_UNROLL):
                j = jb * FFN_UNROLL + u
                xs_v[tile(j), :] = h2_v[tile(idx_ref[0, 0, base + j]), :]
            return 0

        lax.fori_loop(0, tm // FFN_UNROLL, gather, 0)
        xb = jnp.concatenate([xs_v[pl.ds(r, tm, stride=ROW_TILE), :] for r in range(ROW_TILE)],
                             axis=1).astype(BF16)
        a = _dot(xb, wg_ref[0])
        u = _dot(xb, wu_ref[0])
        y = _dot((_silu(a) * u).astype(BF16), wd_ref[0])
        for r in range(ROW_TILE):
            y_v[pl.ds(r, tm, stride=ROW_TILE), :] = y[:, r * LANE:(r + 1) * LANE]

        def scatter(jb, _):
            toks = [idx_ref[0, 0, base + jb * FFN_UNROLL + u] for u in range(FFN_UNROLL)]
            rows = [acc_v[tile(t), :] + y_v[tile(jb * FFN_UNROLL + u), :] * aff_ref[0, 0, t]
                    for u, t in enumerate(toks)]
            for t, row in zip(toks, rows):
                acc_v[tile(t), :] = row
            return 0

        lax.fori_loop(0, tm // FFN_UNROLL, scatter, 0)

    @pl.when(e == pl.num_programs(1) - 1)
    def _():
        cp = pltpu.make_async_copy(acc_v, out_hbm.at[bi], sem.at[1])
        cp.start()
        cp.wait()


def _ffn(idx, aff, h2, w_gate, w_up, w_down):
    b, rows, _ = h2.shape
    s = rows // ROW_TILE
    n_e, d, f = w_gate.shape
    cap = idx.shape[2]
    tm = min(cap, 256)
    assert d == ROW_TILE * LANE and tm % FFN_UNROLL == 0
    smem = lambda n: pl.BlockSpec((1, 1, n), lambda bi, e: (bi * n_e + e, 0, 0), memory_space=pltpu.SMEM)
    idx = idx.reshape(b * n_e, 1, cap)
    aff = aff.reshape(b * n_e, 1, s)
    return pl.pallas_call(
        functools.partial(_ffn_kernel, tm=tm),
        grid=(b, n_e),
        in_specs=[smem(cap), smem(s),
                  pl.BlockSpec(memory_space=pl.ANY),
                  pl.BlockSpec((1, d, f), lambda bi, e: (e, 0, 0)),
                  pl.BlockSpec((1, d, f), lambda bi, e: (e, 0, 0)),
                  pl.BlockSpec((1, f, d), lambda bi, e: (e, 0, 0))],
        out_specs=pl.BlockSpec(memory_space=pl.ANY),
        out_shape=jax.ShapeDtypeStruct((b, rows, LANE), F32),
        scratch_shapes=[pltpu.VMEM((rows, LANE), F32), pltpu.VMEM((rows, LANE), F32),
                        pltpu.VMEM((tm * ROW_TILE, LANE), F32), pltpu.VMEM((tm * ROW_TILE, LANE), F32),
                        pltpu.SemaphoreType.DMA((2,))],
        compiler_params=_cparams(("arbitrary", "arbitrary"), FFN_VMEM_LIMIT),
        name="ffn",
    )(idx, aff, h2, w_gate, w_up, w_down)


def _final_kernel(x1_ref, acc_ref, mod_ref, g_ref, o_ref):
    tm = x1_ref.shape[1]
    acc = jnp.concatenate([acc_ref[0, pl.ds(r, tm, stride=ROW_TILE), :] for r in range(ROW_TILE)], axis=1)
    x2 = x1_ref[0] + mod_ref[0, 5:6, :] * acc
    o_ref[0] = _rms(x2) * g_ref[...]


def _final(x1, acc, mod, g_final):
    b, s, d = x1.shape
    tm = min(s, 512)
    tok = pl.BlockSpec((1, tm, d), lambda bi, i: (bi, i, 0))
    return pl.pallas_call(
        _final_kernel,
        grid=(b, s // tm),
        in_specs=[tok, pl.BlockSpec((1, tm * ROW_TILE, LANE), lambda bi, i: (bi, i, 0)),
                  pl.BlockSpec((1, 6, d), lambda bi, i: (bi, 0, 0)),
                  pl.BlockSpec((1, d), lambda bi, i: (0, 0))],
        out_specs=tok,
        out_shape=jax.ShapeDtypeStruct((b, s, d), F32),
        compiler_params=_cparams(("arbitrary", "arbitrary")),
        name="final",
    )(x1, acc, mod, g_final.reshape(1, d))


def kernel(x, c, ctx, c_ctx, w_ada, b_ada, g_norm_mix, g_norm_ffn, w_in, g_q_lora, g_kv_lora, w_uq, w_ukv,
           ret_exp_fwd, ret_exp_bwd, g_ret, w_o, w_router, w_exp_gate, w_exp_up, w_exp_down, g_final):
    b, s, d = x.shape
    depth = w_ada.shape[0]
    assert depth == 1, "the context stream update is only needed for depth > 1"
    assert s % LANE == 0 and s % GRID_W == 0
    cap = EC_CAPACITY * s // N_EXPERTS

    rows = -(-(b + 1) // 8) * 8
    cc = jnp.zeros((rows, d), F32).at[:b].set(c).at[b].set(c_ctx)
    mods = _ada(cc, w_ada[0], b_ada[0])
    mod = mods[:b].reshape(b, 6, d)
    mod_c = mods[b:b + 1].reshape(1, 6, d)

    w_in_ext, w_uq_ext, w_ukv_ext = _prep_weights(w_in[0], w_uq[0], w_ukv[0])
    tabs = _rope_tables(s)
    gmix = g_norm_mix[0].reshape(1, d)
    gq = g_q_lora[0].reshape(1, -1)
    gkv = g_kv_lora[0].reshape(1, -1)

    qa, ka, va, rq, rk, rv, rg = _inproj(x, mod, gmix, w_in_ext, gq, gkv, w_uq_ext, w_ukv_ext, tabs, is_ctx=False)
    ka_c, va_c, rk_c, rv_c = _inproj(ctx, mod_c, gmix, w_in_ext, gq, gkv, w_uq_ext, w_ukv_ext, None, is_ctx=True)

    o_att = _attention(qa, ka, va, ka_c, va_c)
    y_ret = _retention(rq, rk, rv, rg, rk_c, rv_c, ret_exp_fwd[0], ret_exp_bwd[0], g_ret[0])

    x1, h2, aff = _outproj(o_att, y_ret, x, mod, w_o[0].astype(BF16), g_norm_ffn[0].reshape(1, d),
                           w_router[0].T)
    idx = _topk(aff, cap)
    acc = _ffn(idx, aff, h2, w_exp_gate[0].astype(BF16), w_exp_up[0].astype(BF16), w_exp_down[0].astype(BF16))
    return _final(x1, acc, mod, g_final)
```

```python
import functools

import numpy as np
import jax
import jax.numpy as jnp
from jax import lax
from jax.experimental import pallas as pl
from jax.experimental.pallas import tpu as pltpu

GRID_W = 64
MLA_HEADS = 8
MLA_Q_RANK = 256
MLA_KV_RANK = 128
MLA_NOPE = 64
MLA_ROPE = 32
MLA_V = 64
RET_HEADS = 8
RET_DK = 64
RET_DV = 64
RET_CHUNK = 128
N_EXPERTS = 16
EC_CAPACITY = 2
ROPE_BASE = 10000.0
EPS = 1e-6

LANE = 128
HEAD_TILE = 128
ATTN_TQ = 256
ATTN_SUB = 2
ATTN_KEY_BLOCK = 64
RET_UNROLL = 4
ROW_TILE = 8
FFN_UNROLL = 8
VMEM_LIMIT = 48 * 1024 * 1024
FFN_VMEM_LIMIT = 60 * 1024 * 1024
LOG2E = 1.4426950408889634

F32 = jnp.float32
BF16 = jnp.bfloat16
HIGHEST = lax.Precision.HIGHEST

W_CQ = (0, 256)
W_CKV = (256, 384)
W_KR = (384, 512)
W_RQ = (512, 1024)
W_RK = (1024, 1536)
W_RV = (1536, 2048)
W_RG = (2048, 2560)
D_IN_EXT = 2560


def _cparams(sem, limit=VMEM_LIMIT):
    return pltpu.CompilerParams(dimension_semantics=sem, vmem_limit_bytes=limit)


def _dot(a, b):
    return jnp.dot(a, b, preferred_element_type=F32)


def _dot_nt(a, b):
    return lax.dot_general(a, b, (((1,), (1,)), ((), ())), preferred_element_type=F32)


def _dot_tn(a, b):
    return lax.dot_general(a, b, (((0,), (0,)), ((), ())), preferred_element_type=F32)


def _silu(x):
    return x * jax.nn.sigmoid(x)


def _mla_lane_maps():
    half = MLA_ROPE // 2
    src = np.zeros(HEAD_TILE, np.int32)
    valid = np.zeros(HEAD_TILE, bool)
    src[0:half] = MLA_NOPE + np.arange(half)
    valid[0:half] = True
    src[half:64] = np.arange(64 - half)
    valid[half:64] = True
    src[64:64 + half] = MLA_NOPE + half + np.arange(half)
    valid[64:64 + half] = True
    n_rest = MLA_NOPE - (64 - half)
    src[64 + half:64 + half + n_rest] = (64 - half) + np.arange(n_rest)
    valid[64 + half:64 + half + n_rest] = True
    return src, valid


def _prep_weights(w_in, w_uq, w_ukv):
    d_model = w_in.shape[0]
    half = MLA_ROPE // 2
    src, valid = _mla_lane_maps()
    is_rope = valid & (src >= MLA_NOPE)
    is_nope = valid & (src < MLA_NOPE)

    o = 0
    cq = w_in[:, o:o + MLA_Q_RANK]; o += MLA_Q_RANK
    ckv = w_in[:, o:o + MLA_KV_RANK]; o += MLA_KV_RANK
    kr = w_in[:, o:o + MLA_ROPE]; o += MLA_ROPE
    n_r = RET_HEADS * RET_DK
    rq = w_in[:, o:o + n_r]; o += n_r
    rk = w_in[:, o:o + n_r]; o += n_r
    rv = w_in[:, o:o + RET_HEADS * RET_DV]; o += RET_HEADS * RET_DV
    rg = w_in[:, o:o + RET_HEADS * RET_DV]

    kr_idx = np.where(is_rope, src - MLA_NOPE, 0)
    kr_ext = jnp.where(jnp.asarray(is_rope)[None, :], kr[:, kr_idx], 0.0)

    hd = RET_DK // 2
    lanes = np.arange(n_r)
    tile, l = lanes // LANE, lanes % LANE
    grp = l // hd
    head = 2 * tile + (grp % 2)
    perm = head * RET_DK + (grp // 2) * hd + (l % hd)
    rq_p = rq[:, perm]
    rk_p = rk[:, perm] * (RET_DK ** -0.5)

    w_in_ext = jnp.concatenate([cq, ckv, kr_ext, rq_p, rk_p, rv, rg], axis=1).astype(BF16)
    assert w_in_ext.shape == (d_model, D_IN_EXT)

    dq = MLA_NOPE + MLA_ROPE
    q_cols = (np.arange(MLA_HEADS)[:, None] * dq + src[None, :]).reshape(-1)
    q_valid = np.tile(valid, MLA_HEADS)
    w_uq_ext = jnp.where(jnp.asarray(q_valid)[None, :], w_uq[:, q_cols], 0.0).astype(BF16)

    dkv = MLA_NOPE + MLA_V
    k_cols = (np.arange(MLA_HEADS)[:, None] * dkv + np.where(is_nope, src, 0)[None, :]).reshape(-1)
    k_valid = np.tile(is_nope, MLA_HEADS)
    w_uk_ext = jnp.where(jnp.asarray(k_valid)[None, :], w_ukv[:, k_cols], 0.0)
    v_cols = (np.arange(MLA_HEADS)[:, None] * dkv + MLA_NOPE + np.arange(MLA_V)[None, :]).reshape(-1)
    w_uv = w_ukv[:, v_cols]
    w_ukv_ext = jnp.concatenate([w_uk_ext, w_uv], axis=1).astype(BF16)
    return w_in_ext, w_uq_ext, w_ukv_ext


def _rope_patterns():
    pat = np.zeros((8, LANE), np.float32)
    half = MLA_ROPE // 2
    nf = MLA_ROPE // 4
    inv = ROPE_BASE ** (-np.arange(nf, dtype=np.float64) / nf)
    for base, sign in ((0, -1.0), (64, 1.0)):
        for i in range(half):
            pat[0, base + i] = inv[i % nf]
            pat[1, base + i] = 1.0 if i < nf else 0.0
            pat[2, base + i] = sign
    hd = RET_DK // 2
    nf = RET_DK // 4
    inv = ROPE_BASE ** (-np.arange(nf, dtype=np.float64) / nf)
    for l in range(LANE):
        i = l % hd
        pat[3, l] = inv[i % nf]
        pat[4, l] = 1.0 if i < nf else 0.0
        pat[5, l] = -1.0 if l < 64 else 1.0
    return jnp.asarray(pat)


def _ada_kernel(c_ref, w_ref, b_ref, o_ref):
    s = _silu(c_ref[...])
    o_ref[...] = jnp.dot(s, w_ref[...], preferred_element_type=F32, precision=HIGHEST) + b_ref[...]


def _ada(cc, w_ada, b_ada):
    rows, d = cc.shape
    n = w_ada.shape[1]
    tn = 1024
    return pl.pallas_call(
        _ada_kernel,
        grid=(n // tn,),
        in_specs=[pl.BlockSpec((rows, d), lambda j: (0, 0)),
                  pl.BlockSpec((d, tn), lambda j: (0, j)),
                  pl.BlockSpec((1, tn), lambda j: (0, j))],
        out_specs=pl.BlockSpec((rows, tn), lambda j: (0, j)),
        out_shape=jax.ShapeDtypeStruct((rows, n), F32),
        compiler_params=_cparams(("arbitrary",)),
        name="ada",
    )(cc, w_ada, b_ada.reshape(1, n))


def _rope_kernel(pat_ref, o_ref, *, tm):
    t = pl.program_id(0) * tm + lax.broadcasted_iota(jnp.int32, (tm, LANE), 0)
    row = (t // GRID_W).astype(F32)
    col = (t % GRID_W).astype(F32)
    for k in range(2):
        inv = pat_ref[3 * k:3 * k + 1, :]
        use_row = pat_ref[3 * k + 1:3 * k + 2, :]
        sign = pat_ref[3 * k + 2:3 * k + 3, :]
        ang = jnp.where(use_row > 0.5, row, col) * inv
        active = sign != 0.0
        o_ref[2 * k] = jnp.where(active, jnp.cos(ang), 1.0)
        o_ref[2 * k + 1] = jnp.where(active, sign * jnp.sin(ang), 0.0)


def _rope_tables(seq):
    tm = min(seq, 512)
    return pl.pallas_call(
        functools.partial(_rope_kernel, tm=tm),
        grid=(seq // tm,),
        in_specs=[pl.BlockSpec((8, LANE), lambda i: (0, 0))],
        out_specs=pl.BlockSpec((4, tm, LANE), lambda i: (0, i, 0)),
        out_shape=jax.ShapeDtypeStruct((4, seq, LANE), F32),
        compiler_params=_cparams(("arbitrary",)),
        name="rope",
    )(_rope_patterns())


def _rms(x):
    return x * lax.rsqrt(jnp.mean(x * x, axis=-1, keepdims=True) + EPS)


def _rot(x, cos, sin):
    return x * cos + pltpu.roll(x, 64, axis=1) * sin


def _inproj_kernel(*refs, is_ctx):
    if is_ctx:
        (x_ref, mod_ref, gmix_ref, win_ref, gkv_ref, wukv_ref,
         ka_ref, va_ref, rk_ref, rv_ref) = refs
    else:
        (x_ref, mod_ref, gmix_ref, win_ref, gq_ref, gkv_ref, wuq_ref, wukv_ref, tab_ref,
         qa_ref, ka_ref, va_ref, rq_ref, rk_ref, rv_ref, rg_ref) = refs
    n_k = MLA_HEADS * HEAD_TILE
    h = _rms(x_ref[0]) * gmix_ref[...] * (1.0 + mod_ref[0, 1:2, :]) + mod_ref[0, 0:1, :]
    z = _dot(h.astype(BF16), win_ref[...])
    ckv = _rms(z[:, W_CKV[0]:W_CKV[1]]) * gkv_ref[...]
    kv = _dot(ckv.astype(BF16), wukv_ref[...])
    kr = z[:, W_KR[0]:W_KR[1]]
    if not is_ctx:
        cm, sm, cr, sr = tab_ref[0], tab_ref[1], tab_ref[2], tab_ref[3]
        kr = _rot(kr, cm, sm)
    for hh in range(MLA_HEADS):
        ka_ref[0, :, hh * HEAD_TILE:(hh + 1) * HEAD_TILE] = (
            kv[:, hh * HEAD_TILE:(hh + 1) * HEAD_TILE] + kr).astype(BF16)
    for hp in range(MLA_HEADS // 2):
        va_ref[0, hp, 0] = kv[:, n_k + hp * LANE:n_k + (hp + 1) * LANE].T.astype(BF16)
    rv_ref[0] = z[:, W_RV[0]:W_RV[1]].astype(BF16)
    n_t = (W_RK[1] - W_RK[0]) // LANE
    if is_ctx:
        rk_ref[0] = z[:, W_RK[0]:W_RK[1]].astype(BF16)
        return
    for t in range(n_t):
        rk_ref[0, :, t * LANE:(t + 1) * LANE] = _rot(
            z[:, W_RK[0] + t * LANE:W_RK[0] + (t + 1) * LANE], cr, sr).astype(BF16)
        rq_ref[0, :, t * LANE:(t + 1) * LANE] = _rot(
            z[:, W_RQ[0] + t * LANE:W_RQ[0] + (t + 1) * LANE], cr, sr).astype(BF16)
    rg_ref[0] = z[:, W_RG[0]:W_RG[1]].astype(BF16)
    cq = _rms(z[:, W_CQ[0]:W_CQ[1]]) * gq_ref[...]
    q = _dot(cq.astype(BF16), wuq_ref[...])
    qscale = (MLA_NOPE + MLA_ROPE) ** -0.5 * LOG2E
    for hh in range(MLA_HEADS):
        qh = _rot(q[:, hh * HEAD_TILE:(hh + 1) * HEAD_TILE], cm, sm)
        qa_ref[0, :, hh * HEAD_TILE:(hh + 1) * HEAD_TILE] = (qh * qscale).astype(BF16)


def _inproj(x, mod, gmix, w_in_ext, gq, gkv, w_uq_ext, w_ukv_ext, tabs, *, is_ctx):
    b, s, d = x.shape
    tm = min(s, 512)
    n_k = MLA_HEADS * HEAD_TILE
    n_r = RET_HEADS * RET_DK
    hp = MLA_HEADS // 2
    per_b = mod.shape[0] > 1
    full = lambda shape: pl.BlockSpec(shape, lambda bi, i: tuple(0 for _ in shape))
    tok = lambda n: pl.BlockSpec((1, tm, n), lambda bi, i: (bi, i, 0))
    tok_shape = lambda n: jax.ShapeDtypeStruct((b, s, n), BF16)
    mod_spec = pl.BlockSpec((1, 6, d), (lambda bi, i: (bi, 0, 0)) if per_b else (lambda bi, i: (0, 0, 0)))
    vt_spec = pl.BlockSpec((1, hp, 1, LANE, tm), lambda bi, i: (bi, 0, i, 0, 0))
    vt_shape = jax.ShapeDtypeStruct((b, hp, s // tm, LANE, tm), BF16)
    if is_ctx:
        ins = [x, mod, gmix, w_in_ext, gkv, w_ukv_ext]
        in_specs = [tok(d), mod_spec, full((1, d)), full(w_in_ext.shape), full((1, MLA_KV_RANK)),
                    full(w_ukv_ext.shape)]
        outs = [(tok(n_k), tok_shape(n_k)), (vt_spec, vt_shape)] + [(tok(n_r), tok_shape(n_r))] * 2
    else:
        ins = [x, mod, gmix, w_in_ext, gq, gkv, w_uq_ext, w_ukv_ext, tabs]
        in_specs = [tok(d), mod_spec, full((1, d)), full(w_in_ext.shape), full((1, MLA_Q_RANK)),
                    full((1, MLA_KV_RANK)), full(w_uq_ext.shape), full(w_ukv_ext.shape),
                    pl.BlockSpec((4, tm, LANE), lambda bi, i: (0, i, 0))]
        outs = [(tok(n_k), tok_shape(n_k))] * 2 + [(vt_spec, vt_shape)] + [(tok(n_r), tok_shape(n_r))] * 4
    return pl.pallas_call(
        functools.partial(_inproj_kernel, is_ctx=is_ctx),
        grid=(b, s // tm),
        in_specs=in_specs,
        out_specs=[o[0] for o in outs],
        out_shape=[o[1] for o in outs],
        compiler_params=_cparams(("arbitrary", "arbitrary")),
        name="inproj_ctx" if is_ctx else "inproj",
    )(*ins)


def _attn_kernel(q_ref, kl_ref, kc_ref, vl_ref, vc_ref, o_ref, sa_ref, sb_ref, *, tk, tq):
    n_sub = q_ref.shape[1] // tq
    n_chunks = kl_ref.shape[1] // tk
    lc = kc_ref.shape[1]
    assert n_chunks >= 2 and n_chunks % 2 == 0 and lc <= tk
    streams = tuple((sub * tq, hh * HEAD_TILE) for sub in range(n_sub) for hh in range(2))

    def scores_into(s_ref, k2):
        maxes = []
        for st, (q0, lo) in enumerate(streams):
            s = _dot_nt(k2[:, lo:lo + HEAD_TILE], q_ref[0, q0:q0 + tq, lo:lo + HEAD_TILE])
            s_ref[st, 0:k2.shape[0], :] = s
            maxes.append(jnp.max(s, axis=0, keepdims=True))
        return tuple(maxes)

    def consume(carries, s_ref, n_keys, maxes, vt):
        out = []
        for hh in range(len(streams)):
            m, l, acc = carries[hh]
            m_new = jnp.maximum(m, maxes[hh])
            alpha = jnp.exp2(m - m_new)
            blocks, l_add = [], jnp.zeros_like(l)
            for r in range(0, n_keys, ATTN_KEY_BLOCK):
                p = jnp.exp2(s_ref[hh, r:r + ATTN_KEY_BLOCK, :] - m_new)
                l_add = l_add + jnp.sum(p, axis=0, keepdims=True)
                blocks.append(p.astype(BF16))
            acc = alpha * acc + _dot(vt, jnp.concatenate(blocks, axis=0))
            out.append((m_new, alpha * l + l_add, acc))
        return tuple(out)

    def keys(c):
        return kl_ref[0, pl.ds(pl.multiple_of(c * tk, tk), tk), :]

    def body(i, state):
        carries, max_a = state
        c = 2 * i
        max_b = scores_into(sb_ref, keys(c + 1))
        carries = consume(carries, sa_ref, tk, max_a, vl_ref[0, 0, c])
        max_a = scores_into(sa_ref, keys(c + 2))
        carries = consume(carries, sb_ref, tk, max_b, vl_ref[0, 0, c + 1])
        return carries, max_a

    init = (jnp.full((1, tq), -jnp.inf, F32), jnp.zeros((1, tq), F32), jnp.zeros((LANE, tq), F32))
    state = ((init,) * len(streams), scores_into(sa_ref, kl_ref[0, 0:tk, :]))
    carries, max_a = lax.fori_loop(0, (n_chunks - 2) // 2, body, state)
    max_b = scores_into(sb_ref, kl_ref[0, (n_chunks - 1) * tk:n_chunks * tk, :])
    carries = consume(carries, sa_ref, tk, max_a, vl_ref[0, 0, n_chunks - 2])
    max_c = scores_into(sa_ref, kc_ref[0])
    carries = consume(carries, sb_ref, tk, max_b, vl_ref[0, 0, n_chunks - 1])
    carries = consume(carries, sa_ref, lc, max_c, vc_ref[0, 0, 0])
    for sub in range(n_sub):
        (_, l_a, acc_a), (_, l_b, acc_b) = carries[2 * sub], carries[2 * sub + 1]
        out_t = jnp.concatenate([(acc_a / l_a)[:MLA_V], (acc_b / l_b)[MLA_V:]], axis=0)
        o_ref[0, sub * tq:(sub + 1) * tq, :] = out_t.T.astype(BF16)


def _attention(qa, ka, vt, ka_c, vt_c):
    b, s, _ = qa.shape
    lc = ka_c.shape[1]
    hp, n_chunks, _, tk = vt.shape[1:]
    tq = min(s, ATTN_TQ)
    tqb = min(s, ATTN_SUB * tq)
    n_streams = 2 * (tqb // tq)
    return pl.pallas_call(
        functools.partial(_attn_kernel, tk=tk, tq=tq),
        grid=(b, hp, s // tqb),
        in_specs=[pl.BlockSpec((1, tqb, 2 * HEAD_TILE), lambda bi, h, i: (bi, i, h)),
                  pl.BlockSpec((1, s, 2 * HEAD_TILE), lambda bi, h, i: (bi, 0, h)),
                  pl.BlockSpec((1, lc, 2 * HEAD_TILE), lambda bi, h, i: (bi, 0, h)),
                  pl.BlockSpec((1, 1, n_chunks, LANE, tk), lambda bi, h, i: (bi, h, 0, 0, 0)),
                  pl.BlockSpec((1, 1, 1, LANE, lc), lambda bi, h, i: (bi, h, 0, 0, 0))],
        out_specs=pl.BlockSpec((1, tqb, 2 * MLA_V), lambda bi, h, i: (bi, i, h)),
        out_shape=jax.ShapeDtypeStruct((b, s, MLA_HEADS * MLA_V), BF16),
        scratch_shapes=[pltpu.VMEM((n_streams, tk, tq), F32), pltpu.VMEM((n_streams, tk, tq), F32)],
        compiler_params=_cparams(("arbitrary", "arbitrary", "arbitrary")),
        name="attn",
    )(qa, ka, ka_c, vt, vt_c)


def _ret_kernel(q_ref, k_ref, v_ref, g_ref, kc_ref, vc_ref, eh_ref, el_ref, gret_ref, o_ref, rb_ref):
    c = RET_CHUNK
    n_lat = q_ref.shape[1] // c
    n_ctx = kc_ref.shape[1] // c

    def log_gamma(e):
        return jnp.log1p(-jnp.exp2(-e))

    lgf_a, lgf_b = log_gamma(eh_ref[0, 0:1, :]), log_gamma(eh_ref[0, 1:2, :])
    lgb_a, lgb_b = log_gamma(eh_ref[0, 2:3, :]), log_gamma(eh_ref[0, 3:4, :])
    lgf_l, lgb_l = log_gamma(el_ref[0, 0:1, :]), log_gamma(el_ref[0, 1:2, :])

    ri = lax.broadcasted_iota(jnp.int32, (c, c), 0)
    ci = lax.broadcasted_iota(jnp.int32, (c, c), 1)
    diff = (ri - ci).astype(F32)

    def decay(lgf, lgb):
        return jnp.where(diff >= 0, jnp.exp(lgf * diff), jnp.exp(-lgb * diff))

    d_a, d_b = decay(lgf_a, lgb_a), decay(lgf_b, lgb_b)
    pos = lax.broadcasted_iota(jnp.int32, (c, LANE), 0).astype(F32)
    xi_f = jnp.exp(lgf_l * (pos + 1.0))
    xi_b = jnp.exp(lgb_l * (c - pos))
    zeta_f = jnp.exp(lgf_l * (c - 1.0 - pos))
    zeta_b = jnp.exp(lgb_l * pos)
    hd = RET_DK // 2
    row_is_a = (ri // hd) % 2 == 0
    col_is_a = ci < RET_DV
    gc_f = jnp.exp(jnp.where(row_is_a, lgf_a, lgf_b) * c)
    gc_b = jnp.exp(jnp.where(row_is_a, lgb_a, lgb_b) * c)
    same_head = (row_is_a == col_is_a).astype(F32)
    lane = lax.broadcasted_iota(jnp.int32, (c, LANE), 1)
    qk_is_a = (lane // hd) % 2 == 0
    v_is_a = lane < RET_DV

    def state_update(r, gc, k, v, zeta):
        return gc * r + _dot_tn((k.astype(F32) * zeta).astype(BF16), v)

    def lat(ref, n):
        return ref[0, pl.ds(pl.multiple_of(n * c, c), c), :]

    zero = jnp.zeros((c, LANE), F32)

    r = zero
    for cc in reversed(range(n_ctx)):
        r = state_update(r, gc_b, kc_ref[0, cc * c:(cc + 1) * c, :], vc_ref[0, cc * c:(cc + 1) * c, :], zeta_b)

    def key_value_sum(k, v, zeta):
        return _dot_tn((k.astype(F32) * zeta).astype(BF16), v)

    def bwd_body(i, r):
        ns = [n_lat - 1 - (i * RET_UNROLL + u) for u in range(RET_UNROLL)]
        sums = [key_value_sum(lat(k_ref, n), lat(v_ref, n), zeta_b) for n in ns]
        for n, kv in zip(ns, sums):
            rb_ref[n] = r
            r = gc_b * r + kv
        return r

    assert n_lat % RET_UNROLL == 0
    lax.fori_loop(0, n_lat // RET_UNROLL, bwd_body, r)

    r = zero
    for cc in range(n_ctx):
        r = state_update(r, gc_f, kc_ref[0, cc * c:(cc + 1) * c, :], vc_ref[0, cc * c:(cc + 1) * c, :], zeta_f)

    gret = gret_ref[...]

    def fwd_body(i, r):
        ns = [i * RET_UNROLL + u for u in range(RET_UNROLL)]
        sums = [key_value_sum(lat(k_ref, n), lat(v_ref, n), zeta_f) for n in ns]
        for n, kv in zip(ns, sums):
            chunk_out(n, r)
            r = gc_f * r + kv
        return r

    def chunk_out(n, r):
        q, k, v = lat(q_ref, n), lat(k_ref, n), lat(v_ref, n)
        qz = jnp.zeros_like(q)
        s_a = _dot_nt(jnp.where(qk_is_a, q, qz), k) * d_a
        s_b = _dot_nt(jnp.where(qk_is_a, qz, q), k) * d_b
        vz = jnp.zeros_like(v)
        p2 = jnp.concatenate([s_a, s_b], axis=1).astype(BF16)
        v2 = jnp.concatenate([jnp.where(v_is_a, v, vz), jnp.where(v_is_a, vz, v)], axis=0)
        qf = q.astype(F32)
        q2 = jnp.concatenate([(qf * xi_f).astype(BF16), (qf * xi_b).astype(BF16)], axis=1)
        r2 = jnp.concatenate([(r * same_head).astype(BF16), (rb_ref[n] * same_head).astype(BF16)], axis=0)
        y = _dot(p2, v2) + _dot(q2, r2)
        inv_n = 1.0 / RET_DV
        sum_a = jnp.sum(jnp.where(v_is_a, y, 0.0), axis=-1, keepdims=True)
        sum_b = jnp.sum(jnp.where(v_is_a, 0.0, y), axis=-1, keepdims=True)
        dlt = y - jnp.where(v_is_a, sum_a, sum_b) * inv_n
        sq = dlt * dlt
        var_a = jnp.sum(jnp.where(v_is_a, sq, 0.0), axis=-1, keepdims=True)
        var_b = jnp.sum(jnp.where(v_is_a, 0.0, sq), axis=-1, keepdims=True)
        yn = dlt * lax.rsqrt(jnp.where(v_is_a, var_a, var_b) * inv_n + EPS) * gret
        o_ref[0, pl.ds(pl.multiple_of(n * c, c), c), :] = (yn * _silu(lat(g_ref, n).astype(F32))).astype(BF16)

    lax.fori_loop(0, n_lat // RET_UNROLL, fwd_body, r)


def _retention(rq, rk, rv, rg, rk_c, rv_c, exp_f, exp_b, g_ret):
    b, s, n = rq.shape
    lc = rk_c.shape[1]
    n_t = n // LANE
    ef = exp_f.reshape(n_t, 2)
    eb = exp_b.reshape(n_t, 2)
    eh = jnp.broadcast_to(jnp.concatenate([ef, eb], axis=1)[:, :, None], (n_t, 4, LANE)).astype(F32)
    lane_is_b = (np.arange(LANE) // (RET_DK // 2)) % 2
    el = jnp.stack([ef[:, lane_is_b], eb[:, lane_is_b]], axis=1).astype(F32)
    tok = lambda length: pl.BlockSpec((1, length, LANE), lambda bi, t: (bi, 0, t))
    return pl.pallas_call(
        _ret_kernel,
        grid=(b, n_t),
        in_specs=[tok(s), tok(s), tok(s), tok(s), tok(lc), tok(lc),
                  pl.BlockSpec((1, 4, LANE), lambda bi, t: (t, 0, 0)),
                  pl.BlockSpec((1, 2, LANE), lambda bi, t: (t, 0, 0)),
                  pl.BlockSpec((1, LANE), lambda bi, t: (0, t))],
        out_specs=tok(s),
        out_shape=jax.ShapeDtypeStruct((b, s, n), BF16),
        scratch_shapes=[pltpu.VMEM((s // RET_CHUNK, RET_CHUNK, LANE), F32)],
        compiler_params=_cparams(("arbitrary", "arbitrary")),
        name="ret",
    )(rq, rk, rv, rg, rk_c, rv_c, eh, el, g_ret.reshape(1, n))


def _outproj_kernel(oa_ref, yr_ref, x_ref, mod_ref, wo_ref, gffn_ref, wr_ref, x1_ref, h2_ref, aff_ref):
    n_a = oa_ref.shape[2]
    y = _dot(oa_ref[0], wo_ref[0:n_a, :]) + _dot(yr_ref[0], wo_ref[n_a:, :])
    x1 = x_ref[0] + mod_ref[0, 2:3, :] * y
    x1_ref[0] = x1
    h2 = _rms(x1) * gffn_ref[...] * (1.0 + mod_ref[0, 4:5, :]) + mod_ref[0, 3:4, :]
    tm = h2.shape[0]
    for r in range(ROW_TILE):
        h2_ref[0, pl.ds(r, tm, stride=ROW_TILE), :] = h2[:, r * LANE:(r + 1) * LANE]
    logits =lax.dot_general(wr_ref[...], h2, (((1,), (1,)), ((), ())),
                             preferred_element_type=F32, precision=HIGHEST)
    ex = jnp.exp(logits - jnp.max(logits, axis=0, keepdims=True))
    aff_ref[0] = ex / jnp.sum(ex, axis=0, keepdims=True)


def _outproj(o_att, y_ret, x, mod, w_o, g_ffn, w_router_t):
    b, s, d = x.shape
    tm = min(s, 512)
    n_e = w_router_t.shape[0]
    tok = lambda n: pl.BlockSpec((1, tm, n), lambda bi, i: (bi, i, 0))
    full = lambda shape: pl.BlockSpec(shape, lambda bi, i: tuple(0 for _ in shape))
    return pl.pallas_call(
        _outproj_kernel,
        grid=(b, s // tm),
        in_specs=[tok(o_att.shape[2]), tok(y_ret.shape[2]), tok(d),
                  pl.BlockSpec((1, 6, d), lambda bi, i: (bi, 0, 0)),
                  full(w_o.shape), full((1, d)), full(w_router_t.shape)],
        out_specs=[tok(d), pl.BlockSpec((1, tm * ROW_TILE, LANE), lambda bi, i: (bi, i, 0)),
                   pl.BlockSpec((1, n_e, tm), lambda bi, i: (bi, 0, i))],
        out_shape=[jax.ShapeDtypeStruct((b, s, d), F32), jax.ShapeDtypeStruct((b, s * ROW_TILE, LANE), F32),
                   jax.ShapeDtypeStruct((b, n_e, s), F32)],
        compiler_params=_cparams(("arbitrary", "arbitrary")),
        name="outproj",
    )(o_att, y_ret, x, mod, w_o, g_ffn, w_router_t)


def _topk_kernel(aff_ref, idx_ref, c_ref, t_ref, *, cap):
    n_e, n_r, _ = c_ref.shape
    assert n_r <= LANE
    a = aff_ref[0]

    def count(mask):
        ones = jnp.where(mask, 1, 0)
        return jnp.sum(jnp.sum(ones, axis=1, keepdims=True), axis=2, keepdims=True)

    def as_float(bits):
        return pltpu.bitcast(bits, F32)

    def search(i, lo):
        cand = lo | jnp.left_shift(jnp.int32(1), 30 - i)
        return jnp.where(count(a >= as_float(cand)) >= cap, cand, lo)

    thr = lax.fori_loop(0, 31, search, jnp.zeros((n_e, 1, 1), jnp.int32))
    above = a >= as_float(thr + 1)
    need = cap - count(above)
    window = (a >= as_float(thr)) & jnp.logical_not(above)
    tok = (lax.broadcasted_iota(jnp.int32, (1, n_r, LANE), 1) * LANE
           + lax.broadcasted_iota(jnp.int32, (1, n_r, LANE), 2))

    def surplus(state):
        return jnp.max(state[1] - need) > 0

    def drop_one(state):
        win, n_win = state
        active = n_win > need
        inside = win > 0.5
        low = jnp.min(jnp.min(jnp.where(inside, a, jnp.inf), axis=1, keepdims=True), axis=2, keepdims=True)
        cand = inside & (a == low)
        last = jnp.max(jnp.max(jnp.where(cand, tok, -1), axis=1, keepdims=True), axis=2, keepdims=True)
        win = jnp.where(cand & (tok == last) & active, 0.0, win)
        return win, n_win - jnp.where(active, 1, 0)

    win, _ = lax.while_loop(surplus, drop_one, (jnp.where(window, 1.0, 0.0), count(window)))
    sel = above | (win > 0.5)

    li = lax.broadcasted_iota(jnp.int32, (LANE, LANE), 0)
    lj = lax.broadcasted_iota(jnp.int32, (LANE, LANE), 1)
    tri = jnp.where(li <= lj, 1.0, 0.0).astype(BF16)
    ones = jnp.ones((LANE, LANE), BF16)
    n_flat = n_e * n_r
    m2 = jnp.where(sel, 1.0, 0.0).astype(BF16).reshape(n_flat, LANE)
    within = _dot(m2, tri)
    tot = _dot(m2, ones)
    pi = lax.broadcasted_iota(jnp.int32, (n_flat, n_flat), 0)
    pj = lax.broadcasted_iota(jnp.int32, (n_flat, n_flat), 1)
    before = jnp.where((pi // n_r == pj // n_r) & (pj < pi), 1.0, 0.0).astype(BF16)
    off = _dot(before, tot.astype(BF16))
    c_ref[...] = (within + off).reshape(n_e, n_r, LANE)
    t_ref[...] = tot.reshape(n_e, n_r, LANE)

    incl = jnp.where(li <= lj, 1.0, 0.0).astype(BF16)[:n_r]
    jcol = lax.broadcasted_iota(jnp.int32, (cap, LANE), 0).astype(F32)
    lane = lax.broadcasted_iota(jnp.int32, (cap, LANE), 1)
    ones8 = jnp.ones((8, LANE), BF16)
    pad = jnp.zeros((LANE - n_r, LANE), F32)

    def per_expert(e, _):
        tot_e = t_ref[e].astype(BF16)
        row_end = _dot_tn(tot_e, incl)[0:1, :]
        row_start = row_end - _dot_tn(tot_e, jnp.where(li == lj, 1.0, 0.0).astype(BF16)[:n_r])[0:1, :]
        full_rows = (row_end <= jcol) & (lane < n_r)
        in_row = (row_start <= jcol) & (row_end > jcol) & (lane < n_r)
        c_pad = jnp.concatenate([c_ref[e], pad], axis=0) if n_r < LANE else c_ref[e]
        c_row = jnp.dot(jnp.where(in_row, 1.0, 0.0), c_pad, preferred_element_type=F32, precision=HIGHEST)
        n_rows = _dot_nt(ones8, jnp.where(full_rows, 1.0, 0.0).astype(BF16))
        n_lanes = _dot_nt(ones8, jnp.where(c_row <= jcol, 1.0, 0.0).astype(BF16))
        idx_ref[0, pl.ds(e, 1), :] = (n_rows[0:1, :] * LANE + n_lanes[0:1, :]).astype(jnp.int32)
        return 0

    lax.fori_loop(0, n_e, per_expert, 0)


def _topk(aff, cap):
    b, n_e, s = aff.shape
    n_r = s // LANE
    aff4 = aff.reshape(b, n_e, n_r, LANE)
    return pl.pallas_call(
        functools.partial(_topk_kernel, cap=cap),
        grid=(b,),
        in_specs=[pl.BlockSpec((1, n_e, n_r, LANE), lambda bi: (bi, 0, 0, 0))],
        out_specs=pl.BlockSpec((1, n_e, cap), lambda bi: (bi, 0, 0)),
        out_shape=jax.ShapeDtypeStruct((b, n_e, cap), jnp.int32),
        scratch_shapes=[pltpu.VMEM((n_e, n_r, LANE), F32), pltpu.VMEM((n_e, n_r, LANE), F32)],
        compiler_params=_cparams(("arbitrary",)),
        name="topk",
    )(aff4)


def _ffn_kernel(idx_ref, aff_ref, h2_hbm, wg_ref, wu_ref, wd_ref, out_hbm,
                h2_v, acc_v, sem, *part_bufs, tm):
    bi = pl.program_id(0)
    e = pl.program_id(1)
    cap = idx_ref.shape[2]
    n_parts = cap // tm
    xs_bufs, y_bufs = part_bufs[:n_parts], part_bufs[n_parts:]

    @pl.when(e == 0)
    def _():
        cp = pltpu.make_async_copy(h2_hbm.at[bi], h2_v, sem.at[0])
        cp.start()
        acc_v[...] = jnp.zeros_like(acc_v)
        cp.wait()

    def tile(t):
        return pl.ds(pl.multiple_of(t * ROW_TILE, ROW_TILE), ROW_TILE)

    def gather(part):
        for j in range(tm):
            xs_bufs[part][j * ROW_TILE:(j + 1) * ROW_TILE, :] = h2_v[tile(idx_ref[0, 0, part * tm + j]), :]

    def experts(part):
        xb = jnp.concatenate([xs_bufs[part][pl.ds(r, tm, stride=ROW_TILE), :] for r in range(ROW_TILE)],
                             axis=1).astype(BF16)
        a = _dot(xb, wg_ref[0])
        u = _dot(xb, wu_ref[0])
        y = _dot((_silu(a) * u).astype(BF16), wd_ref[0])
        for r in range(ROW_TILE):
            y_bufs[part][pl.ds(r, tm, stride=ROW_TILE), :] = y[:, r * LANE:(r + 1) * LANE]

    def scatter(part):
        for j0 in range(0, tm, FFN_UNROLL):
            toks = [idx_ref[0, 0, part * tm + j0 + u] for u in range(FFN_UNROLL)]
            rows = [acc_v[tile(t), :]
                    + y_bufs[part][(j0 + u) * ROW_TILE:(j0 + u + 1) * ROW_TILE, :] * aff_ref[0, 0, t]
                    for u, t in enumerate(toks)]
            for t, row in zip(toks, rows):
                acc_v[tile(t), :] = row

    gather(0)
    for part in range(n_parts):
        experts(part)
        if part + 1 < n_parts:
            gather(part + 1)
        scatter(part)

    @pl.when(e == pl.num_programs(1) - 1)
    def _():
        cp = pltpu.make_async_copy(acc_v, out_hbm.at[bi], sem.at[1])
        cp.start()
        cp.wait()


def _ffn(idx, aff, h2, w_gate, w_up, w_down):
    b, rows, _ = h2.shape
    s = rows // ROW_TILE
    n_e, d, f = w_gate.shape
    cap = idx.shape[2]
    tm = min(cap, 256)
    assert d == ROW_TILE * LANE and tm % FFN_UNROLL == 0
    smem = lambda n: pl.BlockSpec((1, 1, n), lambda bi, e: (bi * n_e + e, 0, 0), memory_space=pltpu.SMEM)
    idx = idx.reshape(b * n_e, 1, cap)
    aff = aff.reshape(b * n_e, 1, s)
    return pl.pallas_call(
        functools.partial(_ffn_kernel, tm=tm),
        grid=(b, n_e),
        in_specs=[smem(cap), smem(s),
                  pl.BlockSpec(memory_space=pl.ANY),
                  pl.BlockSpec((1, d, f), lambda bi, e: (e, 0, 0)),
                  pl.BlockSpec((1, d, f), lambda bi, e: (e, 0, 0)),
                  pl.BlockSpec((1, f, d), lambda bi, e: (e, 0, 0))],
        out_specs=pl.BlockSpec(memory_space=pl.ANY),
        out_shape=jax.ShapeDtypeStruct((b, rows, LANE), F32),
        scratch_shapes=[pltpu.VMEM((rows, LANE), F32), pltpu.VMEM((rows, LANE), F32),
                        pltpu.SemaphoreType.DMA((2,))]
        + [pltpu.VMEM((tm * ROW_TILE, LANE), F32)] * (2 * (cap // tm)),
        compiler_params=_cparams(("arbitrary", "arbitrary"), FFN_VMEM_LIMIT),
        name="ffn",
    )(idx, aff, h2, w_gate, w_up, w_down)


def _final_kernel(x1_ref, acc_ref, mod_ref, g_ref, o_ref):
    tm = x1_ref.shape[1]
    acc = jnp.concatenate([acc_ref[0, pl.ds(r, tm, stride=ROW_TILE), :] for r in range(ROW_TILE)], axis=1)
    x2 = x1_ref[0] + mod_ref[0, 5:6, :] * acc
    o_ref[0] = _rms(x2) * g_ref[...]


def _final(x1, acc, mod, g_final):
    b, s, d = x1.shape
    tm = min(s, 512)
    tok = pl.BlockSpec((1, tm, d), lambda bi, i: (bi, i, 0))
    return pl.pallas_call(
        _final_kernel,
        grid=(b, s // tm),
        in_specs=[tok, pl.BlockSpec((1, tm * ROW_TILE, LANE), lambda bi, i: (bi, i, 0)),
                  pl.BlockSpec((1, 6, d), lambda bi, i: (bi, 0, 0)),
                  pl.BlockSpec((1, d), lambda bi, i: (0, 0))],
        out_specs=tok,
        out_shape=jax.ShapeDtypeStruct((b, s, d), F32),
        compiler_params=_cparams(("arbitrary", "arbitrary")),
        name="final",
    )(x1, acc, mod, g_final.reshape(1, d))


def kernel(x, c, ctx, c_ctx, w_ada, b_ada, g_norm_mix, g_norm_ffn, w_in, g_q_lora, g_kv_lora, w_uq, w_ukv,
           ret_exp_fwd, ret_exp_bwd, g_ret, w_o, w_router, w_exp_gate, w_exp_up, w_exp_down, g_final):
    b, s, d = x.shape
    depth = w_ada.shape[0]
    assert depth == 1, "the context stream update is only needed for depth > 1"
    assert s % LANE == 0 and s % GRID_W == 0
    cap = EC_CAPACITY * s // N_EXPERTS

    rows = -(-(b + 1) // 8) * 8
    cc = jnp.zeros((rows, d), F32).at[:b].set(c).at[b].set(c_ctx)
    mods = _ada(cc, w_ada[0], b_ada[0])
    mod = mods[:b].reshape(b, 6, d)
    mod_c = mods[b:b + 1].reshape(1, 6, d)

    w_in_ext, w_uq_ext, w_ukv_ext = _prep_weights(w_in[0], w_uq[0], w_ukv[0])
    tabs = _rope_tables(s)
    gmix = g_norm_mix[0].reshape(1, d)
    gq = g_q_lora[0].reshape(1, -1)
    gkv = g_kv_lora[0].reshape(1, -1)

    qa, ka, va, rq, rk, rv, rg = _inproj(x, mod, gmix, w_in_ext, gq, gkv, w_uq_ext, w_ukv_ext, tabs, is_ctx=False)
    ka_c, va_c, rk_c, rv_c = _inproj(ctx, mod_c, gmix, w_in_ext, gq, gkv, w_uq_ext, w_ukv_ext, None, is_ctx=True)

    o_att = _attention(qa, ka, va, ka_c, va_c)
    y_ret = _retention(rq, rk, rv, rg, rk_c, rv_c, ret_exp_fwd[0], ret_exp_bwd[0], g_ret[0])

    x1, h2, aff = _outproj(o_att, y_ret, x, mod, w_o[0].astype(BF16), g_norm_ffn[0].reshape(1, d),
                           w_router[0].T)
    idx = _topk(aff, cap)
    acc = _ffn(idx, aff, h2, w_exp_gate[0].astype(BF16), w_exp_up[0].astype(BF16), w_exp_down[0].astype(BF16))
    return _final(x1, acc, mod, g_final)
```

```python
import functools

import numpy as np
import jax
import jax.numpy as jnp
from jax import lax
from jax.experimental import pallas as pl
from jax.experimental.pallas import tpu as pltpu

GRID_W = 64
MLA_HEADS = 8
MLA_Q_RANK = 256
MLA_KV_RANK = 128
MLA_NOPE = 64
MLA_ROPE = 32
MLA_V = 64
RET_HEADS = 8
RET_DK = 64
RET_DV = 64
RET_CHUNK = 128
N_EXPERTS = 16
EC_CAPACITY = 2
ROPE_BASE = 10000.0
EPS = 1e-6

LANE = 128
HEAD_TILE = 128
ATTN_TQ = 256
ATTN_SUB = 4
ATTN_KEY_BLOCK = 64
TOPK_GROUP = 4
RET_UNROLL = 8
ROW_TILE = 8
FFN_UNROLL = 8
VMEM_LIMIT = 48 * 1024 * 1024
FFN_VMEM_LIMIT = 60 * 1024 * 1024
LOG2E = 1.4426950408889634

F32 = jnp.float32
BF16 = jnp.bfloat16
HIGHEST = lax.Precision.HIGHEST

W_CQ = (0, 256)
W_CKV = (256, 384)
W_KR = (384, 512)
W_RQ = (512, 1024)
W_RK = (1024, 1536)
W_RV = (1536, 2048)
W_RG = (2048, 2560)
D_IN_EXT = 2560


def _cparams(sem, limit=VMEM_LIMIT):
    return pltpu.CompilerParams(dimension_semantics=sem, vmem_limit_bytes=limit)


def _dot(a, b):
    return jnp.dot(a, b, preferred_element_type=F32)


def _dot_nt(a, b):
    return lax.dot_general(a, b, (((1,), (1,)), ((), ())), preferred_element_type=F32)


def _dot_tn(a, b):
    return lax.dot_general(a, b, (((0,), (0,)), ((), ())), preferred_element_type=F32)


def _silu(x):
    return x * jax.nn.sigmoid(x)


def _mla_lane_maps():
    half = MLA_ROPE // 2
    src = np.zeros(HEAD_TILE, np.int32)
    valid = np.zeros(HEAD_TILE, bool)
    src[0:half] = MLA_NOPE + np.arange(half)
    valid[0:half] = True
    src[half:64] = np.arange(64 - half)
    valid[half:64] = True
    src[64:64 + half] = MLA_NOPE + half + np.arange(half)
    valid[64:64 + half] = True
    n_rest = MLA_NOPE - (64 - half)
    src[64 + half:64 + half + n_rest] = (64 - half) + np.arange(n_rest)
    valid[64 + half:64 + half + n_rest] = True
    return src, valid


def _prep_weights(w_in, w_uq, w_ukv):
    d_model = w_in.shape[0]
    half = MLA_ROPE // 2
    src, valid = _mla_lane_maps()
    is_rope = valid & (src >= MLA_NOPE)
    is_nope = valid & (src < MLA_NOPE)

    o = 0
    cq = w_in[:, o:o + MLA_Q_RANK]; o += MLA_Q_RANK
    ckv = w_in[:, o:o + MLA_KV_RANK]; o += MLA_KV_RANK
    kr = w_in[:, o:o + MLA_ROPE]; o += MLA_ROPE
    n_r = RET_HEADS * RET_DK
    rq = w_in[:, o:o + n_r]; o += n_r
    rk = w_in[:, o:o + n_r]; o += n_r
    rv = w_in[:, o:o + RET_HEADS * RET_DV]; o += RET_HEADS * RET_DV
    rg = w_in[:, o:o + RET_HEADS * RET_DV]

    kr_idx = np.where(is_rope, src - MLA_NOPE, 0)
    kr_ext = jnp.where(jnp.asarray(is_rope)[None, :], kr[:, kr_idx], 0.0)

    hd = RET_DK // 2
    lanes = np.arange(n_r)
    tile, l = lanes // LANE, lanes % LANE
    grp = l // hd
    head = 2 * tile + (grp % 2)
    perm = head * RET_DK + (grp // 2) * hd + (l % hd)
    rq_p = rq[:, perm]
    rk_p = rk[:, perm] * (RET_DK ** -0.5)

    w_in_ext = jnp.concatenate([cq, ckv, kr_ext, rq_p, rk_p, rv, rg], axis=1).astype(BF16)
    assert w_in_ext.shape == (d_model, D_IN_EXT)

    dq = MLA_NOPE + MLA_ROPE
    q_cols = (np.arange(MLA_HEADS)[:, None] * dq + src[None, :]).reshape(-1)
    q_valid = np.tile(valid, MLA_HEADS)
    w_uq_ext = jnp.where(jnp.asarray(q_valid)[None, :], w_uq[:, q_cols], 0.0).astype(BF16)

    dkv = MLA_NOPE + MLA_V
    k_cols = (np.arange(MLA_HEADS)[:, None] * dkv + np.where(is_nope, src, 0)[None, :]).reshape(-1)
    k_valid = np.tile(is_nope, MLA_HEADS)
    w_uk_ext = jnp.where(jnp.asarray(k_valid)[None, :], w_ukv[:, k_cols], 0.0)
    v_cols = (np.arange(MLA_HEADS)[:, None] * dkv + MLA_NOPE + np.arange(MLA_V)[None, :]).reshape(-1)
    w_uv = w_ukv[:, v_cols]
    w_ukv_ext = jnp.concatenate([w_uk_ext, w_uv], axis=1).astype(BF16)
    return w_in_ext, w_uq_ext, w_ukv_ext


def _rope_patterns():
    pat = np.zeros((8, LANE), np.float32)
    half = MLA_ROPE // 2
    nf = MLA_ROPE // 4
    inv = ROPE_BASE ** (-np.arange(nf, dtype=np.float64) / nf)
    for base, sign in ((0, -1.0), (64, 1.0)):
        for i in range(half):
            pat[0, base + i] = inv[i % nf]
            pat[1, base + i] = 1.0 if i < nf else 0.0
            pat[2, base + i] = sign
    hd = RET_DK // 2
    nf = RET_DK // 4
    inv = ROPE_BASE ** (-np.arange(nf, dtype=np.float64) / nf)
    for l in range(LANE):
        i = l % hd
        pat[3, l] = inv[i % nf]
        pat[4, l] = 1.0 if i < nf else 0.0
        pat[5, l] = -1.0 if l < 64 else 1.0
    return jnp.asarray(pat)


def _ada_kernel(c_ref, w_ref, b_ref, o_ref):
    s = _silu(c_ref[...])
    o_ref[...] = jnp.dot(s, w_ref[...], preferred_element_type=F32, precision=HIGHEST) + b_ref[...]


def _ada(cc, w_ada, b_ada):
    rows, d = cc.shape
    n = w_ada.shape[1]
    tn = 1024
    return pl.pallas_call(
        _ada_kernel,
        grid=(n // tn,),
        in_specs=[pl.BlockSpec((rows, d), lambda j: (0, 0)),
                  pl.BlockSpec((d, tn), lambda j: (0, j)),
                  pl.BlockSpec((1, tn), lambda j: (0, j))],
        out_specs=pl.BlockSpec((rows, tn), lambda j: (0, j)),
        out_shape=jax.ShapeDtypeStruct((rows, n), F32),
        compiler_params=_cparams(("arbitrary",)),
        name="ada",
    )(cc, w_ada, b_ada.reshape(1, n))


def _rope_kernel(pat_ref, o_ref, *, tm):
    t = pl.program_id(0) * tm + lax.broadcasted_iota(jnp.int32, (tm, LANE), 0)
    row = (t // GRID_W).astype(F32)
    col = (t % GRID_W).astype(F32)
    for k in range(2):
        inv = pat_ref[3 * k:3 * k + 1, :]
        use_row = pat_ref[3 * k + 1:3 * k + 2, :]
        sign = pat_ref[3 * k + 2:3 * k + 3, :]
        ang = jnp.where(use_row > 0.5, row, col) * inv
        active = sign != 0.0
        o_ref[2 * k] = jnp.where(active, jnp.cos(ang), 1.0)
        o_ref[2 * k + 1] = jnp.where(active, sign * jnp.sin(ang), 0.0)


def _rope_tables(seq):
    tm = min(seq, 512)
    return pl.pallas_call(
        functools.partial(_rope_kernel, tm=tm),
        grid=(seq // tm,),
        in_specs=[pl.BlockSpec((8, LANE), lambda i: (0, 0))],
        out_specs=pl.BlockSpec((4, tm, LANE), lambda i: (0, i, 0)),
        out_shape=jax.ShapeDtypeStruct((4, seq, LANE), F32),
        compiler_params=_cparams(("arbitrary",)),
        name="rope",
    )(_rope_patterns())


def _rms(x):
    return x * lax.rsqrt(jnp.mean(x * x, axis=-1, keepdims=True) + EPS)


def _rot(x, cos, sin):
    return x * cos + pltpu.roll(x, 64, axis=1) * sin


def _inproj_kernel(*refs, is_ctx):
    if is_ctx:
        (x_ref, mod_ref, gmix_ref, win_ref, gkv_ref, wukv_ref,
         ka_ref, va_ref, rk_ref, rv_ref) = refs
    else:
        (x_ref, mod_ref, gmix_ref, win_ref, gq_ref, gkv_ref, wuq_ref, wukv_ref, tab_ref,
         qa_ref, ka_ref, va_ref, rq_ref, rk_ref, rv_ref, rg_ref) = refs
    n_k = MLA_HEADS * HEAD_TILE
    h = _rms(x_ref[0]) * gmix_ref[...] * (1.0 + mod_ref[0, 1:2, :]) + mod_ref[0, 0:1, :]
    z = _dot(h.astype(BF16), win_ref[...])
    ckv = _rms(z[:, W_CKV[0]:W_CKV[1]]) * gkv_ref[...]
    kv = _dot(ckv.astype(BF16), wukv_ref[...])
    kr = z[:, W_KR[0]:W_KR[1]]
    if not is_ctx:
        cm, sm, cr, sr = tab_ref[0], tab_ref[1], tab_ref[2], tab_ref[3]
        kr = _rot(kr, cm, sm)
    for hh in range(MLA_HEADS):
        ka_ref[0, :, hh * HEAD_TILE:(hh + 1) * HEAD_TILE] = (
            kv[:, hh * HEAD_TILE:(hh + 1) * HEAD_TILE] + kr).astype(BF16)
    for hp in range(MLA_HEADS // 2):
        va_ref[0, hp, 0] = kv[:, n_k + hp * LANE:n_k + (hp + 1) * LANE].T.astype(BF16)
    rv_ref[0] = z[:, W_RV[0]:W_RV[1]].astype(BF16)
    n_t = (W_RK[1] - W_RK[0]) // LANE
    if is_ctx:
        rk_ref[0] = z[:, W_RK[0]:W_RK[1]].astype(BF16)
        return
    for t in range(n_t):
        rk_ref[0, :, t * LANE:(t + 1) * LANE] = _rot(
            z[:, W_RK[0] + t * LANE:W_RK[0] + (t + 1) * LANE], cr, sr).astype(BF16)
        rq_ref[0, :, t * LANE:(t + 1) * LANE] = _rot(
            z[:, W_RQ[0] + t * LANE:W_RQ[0] + (t + 1) * LANE], cr, sr).astype(BF16)
    rg_ref[0] = z[:, W_RG[0]:W_RG[1]].astype(BF16)
    cq = _rms(z[:, W_CQ[0]:W_CQ[1]]) * gq_ref[...]
    q = _dot(cq.astype(BF16), wuq_ref[...])
    qscale = (MLA_NOPE + MLA_ROPE) ** -0.5 * LOG2E
    for hh in range(MLA_HEADS):
        qh = _rot(q[:, hh * HEAD_TILE:(hh + 1) * HEAD_TILE], cm, sm)
        qa_ref[0, :, hh * HEAD_TILE:(hh + 1) * HEAD_TILE] = (qh * qscale).astype(BF16)


def _inproj(x, mod, gmix, w_in_ext, gq, gkv, w_uq_ext, w_ukv_ext, tabs, *, is_ctx):
    b, s, d = x.shape
    tm = min(s, 512)
    n_k = MLA_HEADS * HEAD_TILE
    n_r = RET_HEADS * RET_DK
    hp = MLA_HEADS // 2
    per_b = mod.shape[0] > 1
    full = lambda shape: pl.BlockSpec(shape, lambda bi, i: tuple(0 for _ in shape))
    tok = lambda n: pl.BlockSpec((1, tm, n), lambda bi, i: (bi, i, 0))
    tok_shape = lambda n: jax.ShapeDtypeStruct((b, s, n), BF16)
    mod_spec = pl.BlockSpec((1, 6, d), (lambda bi, i: (bi, 0, 0)) if per_b else (lambda bi, i: (0, 0, 0)))
    vt_spec = pl.BlockSpec((1, hp, 1, LANE, tm), lambda bi, i: (bi, 0, i, 0, 0))
    vt_shape = jax.ShapeDtypeStruct((b, hp, s // tm, LANE, tm), BF16)
    if is_ctx:
        ins = [x, mod, gmix, w_in_ext, gkv, w_ukv_ext]
        in_specs = [tok(d), mod_spec, full((1, d)), full(w_in_ext.shape), full((1, MLA_KV_RANK)),
                    full(w_ukv_ext.shape)]
        outs = [(tok(n_k), tok_shape(n_k)), (vt_spec, vt_shape)] + [(tok(n_r), tok_shape(n_r))] * 2
    else:
        ins = [x, mod, gmix, w_in_ext, gq, gkv, w_uq_ext, w_ukv_ext, tabs]
        in_specs = [tok(d), mod_spec, full((1, d)), full(w_in_ext.shape), full((1, MLA_Q_RANK)),
                    full((1, MLA_KV_RANK)), full(w_uq_ext.shape), full(w_ukv_ext.shape),
                    pl.BlockSpec((4, tm, LANE), lambda bi, i: (0, i, 0))]
        outs = [(tok(n_k), tok_shape(n_k))] * 2 + [(vt_spec, vt_shape)] + [(tok(n_r), tok_shape(n_r))] * 4
    return pl.pallas_call(
        functools.partial(_inproj_kernel, is_ctx=is_ctx),
        grid=(b, s // tm),
        in_specs=in_specs,
        out_specs=[o[0] for o in outs],
        out_shape=[o[1] for o in outs],
        compiler_params=_cparams(("arbitrary", "arbitrary")),
        name="inproj_ctx" if is_ctx else "inproj",
    )(*ins)


def _attn_kernel(q_ref, kl_ref, kc_ref, vl_ref, vc_ref, o_ref, sa_ref, sb_ref, *, tk, tq):
    n_sub = q_ref.shape[1] // tq
    n_chunks = kl_ref.shape[1] // tk
    lc = kc_ref.shape[1]
    assert n_chunks >= 2 and n_chunks % 2 == 0 and lc <= tk
    streams = tuple((sub * tq, hh * HEAD_TILE) for sub in range(n_sub) for hh in range(2))

    def scores_into(s_ref, k2):
        maxes = []
        for st, (q0, lo) in enumerate(streams):
            s = _dot_nt(k2[:, lo:lo + HEAD_TILE], q_ref[0, q0:q0 + tq, lo:lo + HEAD_TILE])
            s_ref[st, 0:k2.shape[0], :] = s
            maxes.append(jnp.max(s, axis=0, keepdims=True))
        return tuple(maxes)

    def consume(carries, s_ref, n_keys, maxes, vt):
        out = []
        for hh in range(len(streams)):
            m, l, acc = carries[hh]
            m_new = jnp.maximum(m, maxes[hh])
            alpha = jnp.exp2(m - m_new)
            blocks, l_add = [], jnp.zeros_like(l)
            for r in range(0, n_keys, ATTN_KEY_BLOCK):
                p = jnp.exp2(s_ref[hh, r:r + ATTN_KEY_BLOCK, :] - m_new)
                l_add = l_add + jnp.sum(p, axis=0, keepdims=True)
                blocks.append(p.astype(BF16))
            acc = alpha * acc + _dot(vt, jnp.concatenate(blocks, axis=0))
            out.append((m_new, alpha * l + l_add, acc))
        return tuple(out)

    def keys(c):
        return kl_ref[0, pl.ds(pl.multiple_of(c * tk, tk), tk), :]

    def body(i, state):
        carries, max_a = state
        c = 2 * i
        max_b = scores_into(sb_ref, keys(c + 1))
        carries = consume(carries, sa_ref, tk, max_a, vl_ref[0, 0, c])
        max_a = scores_into(sa_ref, keys(c + 2))
        carries = consume(carries, sb_ref, tk, max_b, vl_ref[0, 0, c + 1])
        return carries, max_a

    init = (jnp.full((1, tq), -jnp.inf, F32), jnp.zeros((1, tq), F32), jnp.zeros((LANE, tq), F32))
    state = ((init,) * len(streams), scores_into(sa_ref, kl_ref[0, 0:tk, :]))
    carries, max_a = lax.fori_loop(0, (n_chunks - 2) // 2, body, state)
    max_b = scores_into(sb_ref, kl_ref[0, (n_chunks - 1) * tk:n_chunks * tk, :])
    carries = consume(carries, sa_ref, tk, max_a, vl_ref[0, 0, n_chunks - 2])
    max_c = scores_into(sa_ref, kc_ref[0])
    carries = consume(carries, sb_ref, tk, max_b, vl_ref[0, 0, n_chunks - 1])
    carries = consume(carries, sa_ref, lc, max_c, vc_ref[0, 0, 0])
    for sub in range(n_sub):
        (_, l_a, acc_a), (_, l_b, acc_b) = carries[2 * sub], carries[2 * sub + 1]
        out_t = jnp.concatenate([(acc_a / l_a)[:MLA_V], (acc_b / l_b)[MLA_V:]], axis=0)
        o_ref[0, sub * tq:(sub + 1) * tq, :] = out_t.T.astype(BF16)


def _attention(qa, ka, vt, ka_c, vt_c):
    b, s, _ = qa.shape
    lc = ka_c.shape[1]
    hp, n_chunks, _, tk = vt.shape[1:]
    tq = min(s, ATTN_TQ)
    tqb = min(s, ATTN_SUB * tq)
    n_streams = 2 * (tqb // tq)
    return pl.pallas_call(
        functools.partial(_attn_kernel, tk=tk, tq=tq),
        grid=(b, hp, s // tqb),
        in_specs=[pl.BlockSpec((1, tqb, 2 * HEAD_TILE), lambda bi, h, i: (bi, i, h)),
                  pl.BlockSpec((1, s, 2 * HEAD_TILE), lambda bi, h, i: (bi, 0, h)),
                  pl.BlockSpec((1, lc, 2 * HEAD_TILE), lambda bi, h, i: (bi, 0, h)),
                  pl.BlockSpec((1, 1, n_chunks, LANE, tk), lambda bi, h, i: (bi, h, 0, 0, 0)),
                  pl.BlockSpec((1, 1, 1, LANE, lc), lambda bi, h, i: (bi, h, 0, 0, 0))],
        out_specs=pl.BlockSpec((1, tqb, 2 * MLA_V), lambda bi, h, i: (bi, i, h)),
        out_shape=jax.ShapeDtypeStruct((b, s, MLA_HEADS * MLA_V), BF16),
        scratch_shapes=[pltpu.VMEM((n_streams, tk, tq), F32), pltpu.VMEM((n_streams, tk, tq), F32)],
        compiler_params=_cparams(("arbitrary", "arbitrary", "arbitrary")),
        name="attn",
    )(qa, ka, ka_c, vt, vt_c)


def _ret_kernel(q_ref, k_ref, v_ref, g_ref, kc_ref, vc_ref, eh_ref, el_ref, gret_ref, o_ref, rb_ref):
    c = RET_CHUNK
    n_lat = q_ref.shape[1] // c
    n_ctx = kc_ref.shape[1] // c

    def log_gamma(e):
        return jnp.log1p(-jnp.exp2(-e))

    lgf_a, lgf_b = log_gamma(eh_ref[0, 0:1, :]), log_gamma(eh_ref[0, 1:2, :])
    lgb_a, lgb_b = log_gamma(eh_ref[0, 2:3, :]), log_gamma(eh_ref[0, 3:4, :])
    lgf_l, lgb_l = log_gamma(el_ref[0, 0:1, :]), log_gamma(el_ref[0, 1:2, :])

    ri = lax.broadcasted_iota(jnp.int32, (c, c), 0)
    ci = lax.broadcasted_iota(jnp.int32, (c, c), 1)
    diff = (ri - ci).astype(F32)

    def decay(lgf, lgb):
        return jnp.where(diff >= 0, jnp.exp(lgf * diff), jnp.exp(-lgb * diff))

    d_a, d_b = decay(lgf_a, lgb_a), decay(lgf_b, lgb_b)
    pos = lax.broadcasted_iota(jnp.int32, (c, LANE), 0).astype(F32)
    xi_f = jnp.exp(lgf_l * (pos + 1.0))
    xi_b = jnp.exp(lgb_l * (c - pos))
    zeta_f = jnp.exp(lgf_l * (c - 1.0 - pos))
    zeta_b = jnp.exp(lgb_l * pos)
    hd = RET_DK // 2
    row_is_a = (ri // hd) % 2 == 0
    col_is_a = ci < RET_DV
    gc_f = jnp.exp(jnp.where(row_is_a, lgf_a, lgf_b) * c)
    gc_b = jnp.exp(jnp.where(row_is_a, lgb_a, lgb_b) * c)
    same_head = (row_is_a == col_is_a).astype(F32)
    lane = lax.broadcasted_iota(jnp.int32, (c, LANE), 1)
    qk_is_a = (lane // hd) % 2 == 0
    v_is_a = lane < RET_DV

    def state_update(r, gc, k, v, zeta):
        return gc * r + _dot_tn((k.astype(F32) * zeta).astype(BF16), v)

    def lat(ref, n):
        return ref[0, pl.ds(pl.multiple_of(n * c, c), c), :]

    zero = jnp.zeros((c, LANE), F32)

    r = zero
    for cc in reversed(range(n_ctx)):
        r = state_update(r, gc_b, kc_ref[0, cc * c:(cc + 1) * c, :], vc_ref[0, cc * c:(cc + 1) * c, :], zeta_b)

    def key_value_sum(k, v, zeta):
        return _dot_tn((k.astype(F32) * zeta).astype(BF16), v)

    def bwd_body(i, r):
        ns = [n_lat - 1 - (i * RET_UNROLL + u) for u in range(RET_UNROLL)]
        sums = [key_value_sum(lat(k_ref, n), lat(v_ref, n), zeta_b) for n in ns]
        for n, kv in zip(ns, sums):
            rb_ref[n] = r
            r = gc_b * r + kv
        return r

    assert n_lat % RET_UNROLL == 0
    lax.fori_loop(0, n_lat // RET_UNROLL, bwd_body, r)

    r = zero
    for cc in range(n_ctx):
        r = state_update(r, gc_f, kc_ref[0, cc * c:(cc + 1) * c, :], vc_ref[0, cc * c:(cc + 1) * c, :], zeta_f)

    gret = gret_ref[...]

    def fwd_body(i, r):
        ns = [i * RET_UNROLL + u for u in range(RET_UNROLL)]
        sums = [key_value_sum(lat(k_ref, n), lat(v_ref, n), zeta_f) for n in ns]
        for n, kv in zip(ns, sums):
            chunk_out(n, r)
            r = gc_f * r + kv
        return r

    def chunk_out(n, r):
        q, k, v = lat(q_ref, n), lat(k_ref, n), lat(v_ref, n)
        qz = jnp.zeros_like(q)
        s_a = _dot_nt(jnp.where(qk_is_a, q, qz), k) * d_a
        s_b = _dot_nt(jnp.where(qk_is_a, qz, q), k) * d_b
        vz = jnp.zeros_like(v)
        p2 = jnp.concatenate([s_a, s_b], axis=1).astype(BF16)
        v2 = jnp.concatenate([jnp.where(v_is_a, v, vz), jnp.where(v_is_a, vz, v)], axis=0)
        qf = q.astype(F32)
        q2 = jnp.concatenate([(qf * xi_f).astype(BF16), (qf * xi_b).astype(BF16)], axis=1)
        r2 = jnp.concatenate([(r * same_head).astype(BF16), (rb_ref[n] * same_head).astype(BF16)], axis=0)
        y = _dot(p2, v2) + _dot(q2, r2)
        inv_n = 1.0 / RET_DV
        sum_a = jnp.sum(jnp.where(v_is_a, y, 0.0), axis=-1, keepdims=True)
        sum_b = jnp.sum(jnp.where(v_is_a, 0.0, y), axis=-1, keepdims=True)
        dlt = y - jnp.where(v_is_a, sum_a, sum_b) * inv_n
        sq = dlt * dlt
        var_a = jnp.sum(jnp.where(v_is_a, sq, 0.0), axis=-1, keepdims=True)
        var_b = jnp.sum(jnp.where(v_is_a, 0.0, sq), axis=-1, keepdims=True)
        yn = dlt * lax.rsqrt(jnp.where(v_is_a, var_a, var_b) * inv_n + EPS) * gret
        o_ref[0, pl.ds(pl.multiple_of(n * c, c), c), :] = (yn * _silu(lat(g_ref, n).astype(F32))).astype(BF16)

    lax.fori_loop(0, n_lat // RET_UNROLL, fwd_body, r)


def _retention(rq, rk, rv, rg, rk_c, rv_c, exp_f, exp_b, g_ret):
    b, s, n = rq.shape
    lc = rk_c.shape[1]
    n_t = n // LANE
    ef = exp_f.reshape(n_t, 2)
    eb = exp_b.reshape(n_t, 2)
    eh = jnp.broadcast_to(jnp.concatenate([ef, eb], axis=1)[:, :, None], (n_t, 4, LANE)).astype(F32)
    lane_is_b = (np.arange(LANE) // (RET_DK // 2)) % 2
    el = jnp.stack([ef[:, lane_is_b], eb[:, lane_is_b]], axis=1).astype(F32)
    tok = lambda length: pl.BlockSpec((1, length, LANE), lambda bi, t: (bi, 0, t))
    return pl.pallas_call(
        _ret_kernel,
        grid=(b, n_t),
        in_specs=[tok(s), tok(s), tok(s), tok(s), tok(lc), tok(lc),
                  pl.BlockSpec((1, 4, LANE), lambda bi, t: (t, 0, 0)),
                  pl.BlockSpec((1, 2, LANE), lambda bi, t: (t, 0, 0)),
                  pl.BlockSpec((1, LANE), lambda bi, t: (0, t))],
        out_specs=tok(s),
        out_shape=jax.ShapeDtypeStruct((b, s, n), BF16),
        scratch_shapes=[pltpu.VMEM((s // RET_CHUNK, RET_CHUNK, LANE), F32)],
        compiler_params=_cparams(("arbitrary", "arbitrary")),
        name="ret",
    )(rq, rk, rv, rg, rk_c, rv_c, eh, el, g_ret.reshape(1, n))


def _outproj_kernel(oa_ref, yr_ref, x_ref, mod_ref, wo_ref, gffn_ref, wr_ref, x1_ref, h2_ref, aff_ref):
    n_a = oa_ref.shape[2]
    y = _dot(oa_ref[0], wo_ref[0:n_a, :]) + _dot(yr_ref[0], wo_ref[n_a:, :])
    x1 = x_ref[0] + mod_ref[0, 2:3, :] * y
    x1_ref[0] = x1
    h2 = _rms(x1) * gffn_ref[...] * (1.0 + mod_ref[0, 4:5, :]) + mod_ref[0, 3:4, :]
    tm = h2.shape[0]
    for r in range(ROW_TILE):
        h2_ref[0, pl.ds(r, tm, stride=ROW_TILE), :] = h2[:, r * LANE:(r + 1) * LANE]
    logits =lax.dot_general(wr_ref[...], h2, (((1,), (1,)), ((), ())),
                             preferred_element_type=F32, precision=HIGHEST)
    ex = jnp.exp(logits - jnp.max(logits, axis=0, keepdims=True))
    aff_ref[0] = ex / jnp.sum(ex, axis=0, keepdims=True)


def _outproj(o_att, y_ret, x, mod, w_o, g_ffn, w_router_t):
    b, s, d = x.shape
    tm = min(s, 512)
    n_e = w_router_t.shape[0]
    tok = lambda n: pl.BlockSpec((1, tm, n), lambda bi, i: (bi, i, 0))
    full = lambda shape: pl.BlockSpec(shape, lambda bi, i: tuple(0 for _ in shape))
    return pl.pallas_call(
        _outproj_kernel,
        grid=(b, s // tm),
        in_specs=[tok(o_att.shape[2]), tok(y_ret.shape[2]), tok(d),
                  pl.BlockSpec((1, 6, d), lambda bi, i: (bi, 0, 0)),
                  full(w_o.shape), full((1, d)), full(w_router_t.shape)],
        out_specs=[tok(d), pl.BlockSpec((1, tm * ROW_TILE, LANE), lambda bi, i: (bi, i, 0)),
                   pl.BlockSpec((1, n_e, tm), lambda bi, i: (bi, 0, i))],
        out_shape=[jax.ShapeDtypeStruct((b, s, d), F32), jax.ShapeDtypeStruct((b, s * ROW_TILE, LANE), F32),
                   jax.ShapeDtypeStruct((b, n_e, s), F32)],
        compiler_params=_cparams(("arbitrary", "arbitrary")),
        name="outproj",
    )(o_att, y_ret, x, mod, w_o, g_ffn, w_router_t)


def _topk_kernel(aff_ref, idx_ref, c_ref, t_ref, *, cap):
    n_e, n_r, _ = c_ref.shape
    assert n_r <= LANE
    a = aff_ref[0]

    def count(mask):
        ones = jnp.where(mask, 1, 0)
        return jnp.sum(jnp.sum(ones, axis=1, keepdims=True), axis=2, keepdims=True)

    def as_float(bits):
        return pltpu.bitcast(bits, F32)

    def search(i, lo):
        cand = lo | jnp.left_shift(jnp.int32(1), 30 - i)
        return jnp.where(count(a >= as_float(cand)) >= cap, cand, lo)

    thr = lax.fori_loop(0, 31, search, jnp.zeros((n_e, 1, 1), jnp.int32))
    above = a >= as_float(thr + 1)
    need = cap - count(above)
    window = (a >= as_float(thr)) & jnp.logical_not(above)
    tok = (lax.broadcasted_iota(jnp.int32, (1, n_r, LANE), 1) * LANE
           + lax.broadcasted_iota(jnp.int32, (1, n_r, LANE), 2))

    def surplus(state):
        return jnp.max(state[1] - need) > 0

    def drop_one(state):
        win, n_win = state
        active = n_win > need
        inside = win > 0.5
        low = jnp.min(jnp.min(jnp.where(inside, a, jnp.inf), axis=1, keepdims=True), axis=2, keepdims=True)
        cand = inside & (a == low)
        last = jnp.max(jnp.max(jnp.where(cand, tok, -1), axis=1, keepdims=True), axis=2, keepdims=True)
        win = jnp.where(cand & (tok == last) & active, 0.0, win)
        return win, n_win - jnp.where(active, 1, 0)

    win, _ = lax.while_loop(surplus, drop_one, (jnp.where(window, 1.0, 0.0), count(window)))
    sel = above | (win > 0.5)

    li = lax.broadcasted_iota(jnp.int32, (LANE, LANE), 0)
    lj = lax.broadcasted_iota(jnp.int32, (LANE, LANE), 1)
    tri = jnp.where(li <= lj, 1.0, 0.0).astype(BF16)
    ones = jnp.ones((LANE, LANE), BF16)
    n_flat = n_e * n_r
    m2 = jnp.where(sel, 1.0, 0.0).astype(BF16).reshape(n_flat, LANE)
    within = _dot(m2, tri)
    tot = _dot(m2, ones)
    pi = lax.broadcasted_iota(jnp.int32, (n_flat, n_flat), 0)
    pj = lax.broadcasted_iota(jnp.int32, (n_flat, n_flat), 1)
    before = jnp.where((pi // n_r == pj // n_r) & (pj < pi), 1.0, 0.0).astype(BF16)
    off = _dot(before, tot.astype(BF16))
    c_ref[...] = (within + off).reshape(n_e, n_r, LANE)
    t_ref[...] = tot.reshape(n_e, n_r, LANE)

    incl = jnp.where(li <= lj, 1.0, 0.0).astype(BF16)[:n_r]
    jcol = lax.broadcasted_iota(jnp.int32, (cap, LANE), 0).astype(F32)
    lane = lax.broadcasted_iota(jnp.int32, (cap, LANE), 1)
    ones8 = jnp.ones((8, LANE), BF16)
    pad = jnp.zeros((LANE - n_r, LANE), F32)

    eye = jnp.where(li == lj, 1.0, 0.0).astype(BF16)[:n_r]

    def expert_group(i, _):
        es = [i * TOPK_GROUP + u for u in range(TOPK_GROUP)]
        tots = [t_ref[e].astype(BF16) for e in es]
        ends = [_dot_tn(t, incl)[0:1, :] for t in tots]
        starts = [end - _dot_tn(t, eye)[0:1, :] for t, end in zip(tots, ends)]
        in_rows = [(start <= jcol) & (end > jcol) & (lane < n_r) for start, end in zip(starts, ends)]
        c_pads = [jnp.concatenate([c_ref[e], pad], axis=0) if n_r < LANE else c_ref[e] for e in es]
        c_rows = [jnp.dot(jnp.where(m, 1.0, 0.0), c, preferred_element_type=F32, precision=HIGHEST)
                  for m, c in zip(in_rows, c_pads)]
        n_rows = [_dot_nt(ones8, jnp.where((end <= jcol) & (lane < n_r), 1.0, 0.0).astype(BF16)) for end in ends]
        n_lanes = [_dot_nt(ones8, jnp.where(c <= jcol, 1.0, 0.0).astype(BF16)) for c in c_rows]
        for e, nr, nl in zip(es, n_rows, n_lanes):
            idx_ref[0, pl.ds(e, 1), :] = (nr[0:1, :] * LANE + nl[0:1, :]).astype(jnp.int32)
        return 0

    assert n_e % TOPK_GROUP == 0
    lax.fori_loop(0, n_e // TOPK_GROUP, expert_group, 0)


def _topk(aff, cap):
    b, n_e, s = aff.shape
    n_r = s // LANE
    aff4 = aff.reshape(b, n_e, n_r, LANE)
    return pl.pallas_call(
        functools.partial(_topk_kernel, cap=cap),
        grid=(b,),
        in_specs=[pl.BlockSpec((1, n_e, n_r, LANE), lambda bi: (bi, 0, 0, 0))],
        out_specs=pl.BlockSpec((1, n_e, cap), lambda bi: (bi, 0, 0)),
        out_shape=jax.ShapeDtypeStruct((b, n_e, cap), jnp.int32),
        scratch_shapes=[pltpu.VMEM((n_e, n_r, LANE), F32), pltpu.VMEM((n_e, n_r, LANE), F32)],
        compiler_params=_cparams(("arbitrary",)),
        name="topk",
    )(aff4)


def _ffn_kernel(idx_ref, aff_ref, h2_hbm, wg_ref, wu_ref, wd_ref, out_hbm,
                h2_v, acc_v, sem, *part_bufs, tm):
    bi = pl.program_id(0)
    e = pl.program_id(1)
    cap = idx_ref.shape[2]
    n_parts = cap // tm
    xs_bufs, y_bufs = part_bufs[:n_parts], part_bufs[n_parts:]

    @pl.when(e == 0)
    def _():
        cp = pltpu.make_async_copy(h2_hbm.at[bi], h2_v, sem.at[0])
        cp.start()
        acc_v[...] = jnp.zeros_like(acc_v)
        cp.wait()

    def tile(t):
        return pl.ds(pl.multiple_of(t * ROW_TILE, ROW_TILE), ROW_TILE)

    def gather(part):
        for j in range(tm):
            xs_bufs[part][j * ROW_TILE:(j + 1) * ROW_TILE, :] = h2_v[tile(idx_ref[0, 0, part * tm + j]), :]

    def experts(part):
        xb = jnp.concatenate([xs_bufs[part][pl.ds(r, tm, stride=ROW_TILE), :] for r in range(ROW_TILE)],
                             axis=1).astype(BF16)
        a = _dot(xb, wg_ref[0])
        u = _dot(xb, wu_ref[0])
        y = _dot((_silu(a) * u).astype(BF16), wd_ref[0])
        for r in range(ROW_TILE):
            y_bufs[part][pl.ds(r, tm, stride=ROW_TILE), :] = y[:, r * LANE:(r + 1) * LANE]

    def scatter(part):
        for j0 in range(0, tm, FFN_UNROLL):
            toks = [idx_ref[0, 0, part * tm + j0 + u] for u in range(FFN_UNROLL)]
            rows = [acc_v[tile(t), :]
                    + y_bufs[part][(j0 + u) * ROW_TILE:(j0 + u + 1) * ROW_TILE, :] * aff_ref[0, 0, t]
                    for u, t in enumerate(toks)]
            for t, row in zip(toks, rows):
                acc_v[tile(t), :] = row

    gather(0)
    for part in range(n_parts):
        experts(part)
        if part + 1 < n_parts:
            gather(part + 1)
        scatter(part)

    @pl.when(e == pl.num_programs(1) - 1)
    def _():
        cp = pltpu.make_async_copy(acc_v, out_hbm.at[bi], sem.at[1])
        cp.start()
        cp.wait()


def _ffn(idx, aff, h2, w_gate, w_up, w_down):
    b, rows, _ = h2.shape
    s = rows // ROW_TILE
    n_e, d, f = w_gate.shape
    cap = idx.shape[2]
    tm = min(cap, 256)
    assert d == ROW_TILE * LANE and tm % FFN_UNROLL == 0
    smem = lambda n: pl.BlockSpec((1, 1, n), lambda bi, e: (bi * n_e + e, 0, 0), memory_space=pltpu.SMEM)
    idx = idx.reshape(b * n_e, 1, cap)
    aff = aff.reshape(b * n_e, 1, s)
    return pl.pallas_call(
        functools.partial(_ffn_kernel, tm=tm),
        grid=(b, n_e),
        in_specs=[smem(cap), smem(s),
                  pl.BlockSpec(memory_space=pl.ANY),
                  pl.BlockSpec((1, d, f), lambda bi, e: (e, 0, 0)),
                  pl.BlockSpec((1, d, f), lambda bi, e: (e, 0, 0)),
                  pl.BlockSpec((1, f, d), lambda bi, e: (e, 0, 0))],
        out_specs=pl.BlockSpec(memory_space=pl.ANY),
        out_shape=jax.ShapeDtypeStruct((b, rows, LANE), F32),
        scratch_shapes=[pltpu.VMEM((rows, LANE), F32), pltpu.VMEM((rows, LANE), F32),
                        pltpu.SemaphoreType.DMA((2,))]
        + [pltpu.VMEM((tm * ROW_TILE, LANE), F32)] * (2 * (cap // tm)),
        compiler_params=_cparams(("arbitrary", "arbitrary"), FFN_VMEM_LIMIT),
        name="ffn",
    )(idx, aff, h2, w_gate, w_up, w_down)


def _final_kernel(x1_ref, acc_ref, mod_ref, g_ref, o_ref):
    tm = x1_ref.shape[1]
    acc = jnp.concatenate([acc_ref[0, pl.ds(r, tm, stride=ROW_TILE), :] for r in range(ROW_TILE)], axis=1)
    x2 = x1_ref[0] + mod_ref[0, 5:6, :] * acc
    o_ref[0] = _rms(x2) * g_ref[...]


def _final(x1, acc, mod, g_final):
    b, s, d = x1.shape
    tm = min(s, 512)
    tok = pl.BlockSpec((1, tm, d), lambda bi, i: (bi, i, 0))
    return pl.pallas_call(
        _final_kernel,
        grid=(b, s // tm),
        in_specs=[tok, pl.BlockSpec((1, tm * ROW_TILE, LANE), lambda bi, i: (bi, i, 0)),
                  pl.BlockSpec((1, 6, d), lambda bi, i: (bi, 0, 0)),
                  pl.BlockSpec((1, d), lambda bi, i: (0, 0))],
        out_specs=tok,
        out_shape=jax.ShapeDtypeStruct((b, s, d), F32),
        compiler_params=_cparams(("arbitrary", "arbitrary")),
        name="final",
    )(x1, acc, mod, g_final.reshape(1, d))


def kernel(x, c, ctx, c_ctx, w_ada, b_ada, g_norm_mix, g_norm_ffn, w_in, g_q_lora, g_kv_lora, w_uq, w_ukv,
           ret_exp_fwd, ret_exp_bwd, g_ret, w_o, w_router, w_exp_gate, w_exp_up, w_exp_down, g_final):
    b, s, d = x.shape
    depth = w_ada.shape[0]
    assert depth == 1, "the context stream update is only needed for depth > 1"
    assert s % LANE == 0 and s % GRID_W == 0
    cap = EC_CAPACITY * s // N_EXPERTS

    rows = -(-(b + 1) // 8) * 8
    cc = jnp.zeros((rows, d), F32).at[:b].set(c).at[b].set(c_ctx)
    mods = _ada(cc, w_ada[0], b_ada[0])
    mod = mods[:b].reshape(b, 6, d)
    mod_c = mods[b:b + 1].reshape(1, 6, d)

    w_in_ext, w_uq_ext, w_ukv_ext = _prep_weights(w_in[0], w_uq[0], w_ukv[0])
    tabs = _rope_tables(s)
    gmix = g_norm_mix[0].reshape(1, d)
    gq = g_q_lora[0].reshape(1, -1)
    gkv = g_kv_lora[0].reshape(1, -1)

    qa, ka, va, rq, rk, rv, rg = _inproj(x, mod, gmix, w_in_ext, gq, gkv, w_uq_ext, w_ukv_ext, tabs, is_ctx=False)
    ka_c, va_c, rk_c, rv_c = _inproj(ctx, mod_c, gmix, w_in_ext, gq, gkv, w_uq_ext, w_ukv_ext, None, is_ctx=True)

    o_att = _attention(qa, ka, va, ka_c, va_c)
    y_ret = _retention(rq, rk, rv, rg, rk_c, rv_c, ret_exp_fwd[0], ret_exp_bwd[0], g_ret[0])

    x1, h2, aff = _outproj(o_att, y_ret, x, mod, w_o[0].astype(BF16), g_norm_ffn[0].reshape(1, d),
                           w_router[0].T)
    idx = _topk(aff, cap)
    acc = _ffn(idx, aff, h2, w_exp_gate[0].astype(BF16), w_exp_up[0].astype(BF16), w_exp_down[0].astype(BF16))
    return _final(x1, acc, mod, g_final)
```

```python
import functools

import numpy as np
import jax
import jax.numpy as jnp
from jax import lax
from jax.experimental import pallas as pl
from jax.experimental.pallas import tpu as pltpu

GRID_W = 64
MLA_HEADS = 8
MLA_Q_RANK = 256
MLA_KV_RANK = 128
MLA_NOPE = 64
MLA_ROPE = 32
MLA_V = 64
RET_HEADS = 8
RET_DK = 64
RET_DV = 64
RET_CHUNK = 128
N_EXPERTS = 16
EC_CAPACITY = 2
ROPE_BASE = 10000.0
EPS = 1e-6

LANE = 128
HEAD_TILE = 128
ATTN_TQ = 256
ATTN_SUB = 8
ATTN_KEY_BLOCK = 64
TOPK_GROUP = 4
RET_UNROLL = 8
ROW_TILE = 8
FFN_UNROLL = 8
VMEM_LIMIT = 48 * 1024 * 1024
FFN_VMEM_LIMIT = 60 * 1024 * 1024
LOG2E = 1.4426950408889634

F32 = jnp.float32
BF16 = jnp.bfloat16
HIGHEST = lax.Precision.HIGHEST

W_CQ = (0, 256)
W_CKV = (256, 384)
W_KR = (384, 512)
W_RQ = (512, 1024)
W_RK = (1024, 1536)
W_RV = (1536, 2048)
W_RG = (2048, 2560)
D_IN_EXT = 2560


def _cparams(sem, limit=VMEM_LIMIT):
    return pltpu.CompilerParams(dimension_semantics=sem, vmem_limit_bytes=limit)


def _dot(a, b):
    return jnp.dot(a, b, preferred_element_type=F32)


def _dot_nt(a, b):
    return lax.dot_general(a, b, (((1,), (1,)), ((), ())), preferred_element_type=F32)


def _dot_tn(a, b):
    return lax.dot_general(a, b, (((0,), (0,)), ((), ())), preferred_element_type=F32)


def _silu(x):
    return x * jax.nn.sigmoid(x)


def _mla_lane_maps():
    half = MLA_ROPE // 2
    src = np.zeros(HEAD_TILE, np.int32)
    valid = np.zeros(HEAD_TILE, bool)
    src[0:half] = MLA_NOPE + np.arange(half)
    valid[0:half] = True
    src[half:64] = np.arange(64 - half)
    valid[half:64] = True
    src[64:64 + half] = MLA_NOPE + half + np.arange(half)
    valid[64:64 + half] = True
    n_rest = MLA_NOPE - (64 - half)
    src[64 + half:64 + half + n_rest] = (64 - half) + np.arange(n_rest)
    valid[64 + half:64 + half + n_rest] = True
    return src, valid


def _prep_weights(w_in, w_uq, w_ukv):
    d_model = w_in.shape[0]
    half = MLA_ROPE // 2
    src, valid = _mla_lane_maps()
    is_rope = valid & (src >= MLA_NOPE)
    is_nope = valid & (src < MLA_NOPE)

    o = 0
    cq = w_in[:, o:o + MLA_Q_RANK]; o += MLA_Q_RANK
    ckv = w_in[:, o:o + MLA_KV_RANK]; o += MLA_KV_RANK
    kr = w_in[:, o:o + MLA_ROPE]; o += MLA_ROPE
    n_r = RET_HEADS * RET_DK
    rq = w_in[:, o:o + n_r]; o += n_r
    rk = w_in[:, o:o + n_r]; o += n_r
    rv = w_in[:, o:o + RET_HEADS * RET_DV]; o += RET_HEADS * RET_DV
    rg = w_in[:, o:o + RET_HEADS * RET_DV]

    kr_idx = np.where(is_rope, src - MLA_NOPE, 0)
    kr_ext = jnp.where(jnp.asarray(is_rope)[None, :], kr[:, kr_idx], 0.0)

    hd = RET_DK // 2
    lanes = np.arange(n_r)
    tile, l = lanes // LANE, lanes % LANE
    grp = l // hd
    head = 2 * tile + (grp % 2)
    perm = head * RET_DK + (grp // 2) * hd + (l % hd)
    rq_p = rq[:, perm]
    rk_p = rk[:, perm] * (RET_DK ** -0.5)

    w_in_ext = jnp.concatenate([cq, ckv, kr_ext, rq_p, rk_p, rv, rg], axis=1).astype(BF16)
    assert w_in_ext.shape == (d_model, D_IN_EXT)

    dq = MLA_NOPE + MLA_ROPE
    q_cols = (np.arange(MLA_HEADS)[:, None] * dq + src[None, :]).reshape(-1)
    q_valid = np.tile(valid, MLA_HEADS)
    w_uq_ext = jnp.where(jnp.asarray(q_valid)[None, :], w_uq[:, q_cols], 0.0).astype(BF16)

    dkv = MLA_NOPE + MLA_V
    k_cols = (np.arange(MLA_HEADS)[:, None] * dkv + np.where(is_nope, src, 0)[None, :]).reshape(-1)
    k_valid = np.tile(is_nope, MLA_HEADS)
    w_uk_ext = jnp.where(jnp.asarray(k_valid)[None, :], w_ukv[:, k_cols], 0.0)
    v_cols = (np.arange(MLA_HEADS)[:, None] * dkv + MLA_NOPE + np.arange(MLA_V)[None, :]).reshape(-1)
    w_uv = w_ukv[:, v_cols]
    w_ukv_ext = jnp.concatenate([w_uk_ext, w_uv], axis=1).astype(BF16)
    return w_in_ext, w_uq_ext, w_ukv_ext


def _rope_patterns():
    pat = np.zeros((8, LANE), np.float32)
    half = MLA_ROPE // 2
    nf = MLA_ROPE // 4
    inv = ROPE_BASE ** (-np.arange(nf, dtype=np.float64) / nf)
    for base, sign in ((0, -1.0), (64, 1.0)):
        for i in range(half):
            pat[0, base + i] = inv[i % nf]
            pat[1, base + i] = 1.0 if i < nf else 0.0
            pat[2, base + i] = sign
    hd = RET_DK // 2
    nf = RET_DK // 4
    inv = ROPE_BASE ** (-np.arange(nf, dtype=np.float64) / nf)
    for l in range(LANE):
        i = l % hd
        pat[3, l] = inv[i % nf]
        pat[4, l] = 1.0 if i < nf else 0.0
        pat[5, l] = -1.0 if l < 64 else 1.0
    return jnp.asarray(pat)


def _ada_kernel(c_ref, w_ref, b_ref, o_ref):
    s = _silu(c_ref[...])
    o_ref[...] = jnp.dot(s, w_ref[...], preferred_element_type=F32, precision=HIGHEST) + b_ref[...]


def _ada(cc, w_ada, b_ada):
    rows, d = cc.shape
    n = w_ada.shape[1]
    tn = 1024
    return pl.pallas_call(
        _ada_kernel,
        grid=(n // tn,),
        in_specs=[pl.BlockSpec((rows, d), lambda j: (0, 0)),
                  pl.BlockSpec((d, tn), lambda j: (0, j)),
                  pl.BlockSpec((1, tn), lambda j: (0, j))],
        out_specs=pl.BlockSpec((rows, tn), lambda j: (0, j)),
        out_shape=jax.ShapeDtypeStruct((rows, n), F32),
        compiler_params=_cparams(("arbitrary",)),
        name="ada",
    )(cc, w_ada, b_ada.reshape(1, n))


def _rope_kernel(pat_ref, o_ref, *, tm):
    t = pl.program_id(0) * tm + lax.broadcasted_iota(jnp.int32, (tm, LANE), 0)
    row = (t // GRID_W).astype(F32)
    col = (t % GRID_W).astype(F32)
    for k in range(2):
        inv = pat_ref[3 * k:3 * k + 1, :]
        use_row = pat_ref[3 * k + 1:3 * k + 2, :]
        sign = pat_ref[3 * k + 2:3 * k + 3, :]
        ang = jnp.where(use_row > 0.5, row, col) * inv
        active = sign != 0.0
        o_ref[2 * k] = jnp.where(active, jnp.cos(ang), 1.0)
        o_ref[2 * k + 1] = jnp.where(active, sign * jnp.sin(ang), 0.0)


def _rope_tables(seq):
    tm = min(seq, 512)
    return pl.pallas_call(
        functools.partial(_rope_kernel, tm=tm),
        grid=(seq // tm,),
        in_specs=[pl.BlockSpec((8, LANE), lambda i: (0, 0))],
        out_specs=pl.BlockSpec((4, tm, LANE), lambda i: (0, i, 0)),
        out_shape=jax.ShapeDtypeStruct((4, seq, LANE), F32),
        compiler_params=_cparams(("arbitrary",)),
        name="rope",
    )(_rope_patterns())


def _rms(x):
    return x * lax.rsqrt(jnp.mean(x * x, axis=-1, keepdims=True) + EPS)


def _rot(x, cos, sin):
    return x * cos + pltpu.roll(x, 64, axis=1) * sin


def _inproj_kernel(*refs, is_ctx):
    if is_ctx:
        (x_ref, mod_ref, gmix_ref, win_ref, gkv_ref, wukv_ref,
         ka_ref, va_ref, rk_ref, rv_ref) = refs
    else:
        (x_ref, mod_ref, gmix_ref, win_ref, gq_ref, gkv_ref, wuq_ref, wukv_ref, tab_ref,
         qa_ref, ka_ref, va_ref, rq_ref, rk_ref, rv_ref, rg_ref) = refs
    n_k = MLA_HEADS * HEAD_TILE
    h = _rms(x_ref[0]) * gmix_ref[...] * (1.0 + mod_ref[0, 1:2, :]) + mod_ref[0, 0:1, :]
    z = _dot(h.astype(BF16), win_ref[...])
    ckv = _rms(z[:, W_CKV[0]:W_CKV[1]]) * gkv_ref[...]
    kv = _dot(ckv.astype(BF16), wukv_ref[...])
    kr = z[:, W_KR[0]:W_KR[1]]
    if not is_ctx:
        cm, sm, cr, sr = tab_ref[0], tab_ref[1], tab_ref[2], tab_ref[3]
        kr = _rot(kr, cm, sm)
    for hh in range(MLA_HEADS):
        ka_ref[0, :, hh * HEAD_TILE:(hh + 1) * HEAD_TILE] = (
            kv[:, hh * HEAD_TILE:(hh + 1) * HEAD_TILE] + kr).astype(BF16)
    for hp in range(MLA_HEADS // 2):
        va_ref[0, hp, 0] = kv[:, n_k + hp * LANE:n_k + (hp + 1) * LANE].T.astype(BF16)
    rv_ref[0] = z[:, W_RV[0]:W_RV[1]].astype(BF16)
    n_t = (W_RK[1] - W_RK[0]) // LANE
    if is_ctx:
        rk_ref[0] = z[:, W_RK[0]:W_RK[1]].astype(BF16)
        return
    for t in range(n_t):
        rk_ref[0, :, t * LANE:(t + 1) * LANE] = _rot(
            z[:, W_RK[0] + t * LANE:W_RK[0] + (t + 1) * LANE], cr, sr).astype(BF16)
        rq_ref[0, :, t * LANE:(t + 1) * LANE] = _rot(
            z[:, W_RQ[0] + t * LANE:W_RQ[0] + (t + 1) * LANE], cr, sr).astype(BF16)
    rg_ref[0] = z[:, W_RG[0]:W_RG[1]].astype(BF16)
    cq = _rms(z[:, W_CQ[0]:W_CQ[1]]) * gq_ref[...]
    q = _dot(cq.astype(BF16), wuq_ref[...])
    qscale = (MLA_NOPE + MLA_ROPE) ** -0.5 * LOG2E
    for hh in range(MLA_HEADS):
        qh = _rot(q[:, hh * HEAD_TILE:(hh + 1) * HEAD_TILE], cm, sm)
        qa_ref[0, :, hh * HEAD_TILE:(hh + 1) * HEAD_TILE] = (qh * qscale).astype(BF16)


def _inproj(x, mod, gmix, w_in_ext, gq, gkv, w_uq_ext, w_ukv_ext, tabs, *, is_ctx):
    b, s, d = x.shape
    tm = min(s, 512)
    n_k = MLA_HEADS * HEAD_TILE
    n_r = RET_HEADS * RET_DK
    hp = MLA_HEADS // 2
    per_b = mod.shape[0] > 1
    full = lambda shape: pl.BlockSpec(shape, lambda bi, i: tuple(0 for _ in shape))
    tok = lambda n: pl.BlockSpec((1, tm, n), lambda bi, i: (bi, i, 0))
    tok_shape = lambda n: jax.ShapeDtypeStruct((b, s, n), BF16)
    mod_spec = pl.BlockSpec((1, 6, d), (lambda bi, i: (bi, 0, 0)) if per_b else (lambda bi, i: (0, 0, 0)))
    vt_spec = pl.BlockSpec((1, hp, 1, LANE, tm), lambda bi, i: (bi, 0, i, 0, 0))
    vt_shape = jax.ShapeDtypeStruct((b, hp, s // tm, LANE, tm), BF16)
    if is_ctx:
        ins = [x, mod, gmix, w_in_ext, gkv, w_ukv_ext]
        in_specs = [tok(d), mod_spec, full((1, d)), full(w_in_ext.shape), full((1, MLA_KV_RANK)),
                    full(w_ukv_ext.shape)]
        outs = [(tok(n_k), tok_shape(n_k)), (vt_spec, vt_shape)] + [(tok(n_r), tok_shape(n_r))] * 2
    else:
        ins = [x, mod, gmix, w_in_ext, gq, gkv, w_uq_ext, w_ukv_ext, tabs]
        in_specs = [tok(d), mod_spec, full((1, d)), full(w_in_ext.shape), full((1, MLA_Q_RANK)),
                    full((1, MLA_KV_RANK)), full(w_uq_ext.shape), full(w_ukv_ext.shape),
                    pl.BlockSpec((4, tm, LANE), lambda bi, i: (0, i, 0))]
        outs = [(tok(n_k), tok_shape(n_k))] * 2 + [(vt_spec, vt_shape)] + [(tok(n_r), tok_shape(n_r))] * 4
    return pl.pallas_call(
        functools.partial(_inproj_kernel, is_ctx=is_ctx),
        grid=(b, s // tm),
        in_specs=in_specs,
        out_specs=[o[0] for o in outs],
        out_shape=[o[1] for o in outs],
        compiler_params=_cparams(("arbitrary", "arbitrary")),
        name="inproj_ctx" if is_ctx else "inproj",
    )(*ins)


def _attn_kernel(q_ref, kl_ref, kc_ref, vl_ref, vc_ref, o_ref, sa_ref, sb_ref, *, tk, tq):
    n_sub = q_ref.shape[1] // tq
    n_chunks = kl_ref.shape[1] // tk
    lc = kc_ref.shape[1]
    assert n_chunks >= 2 and n_chunks % 2 == 0 and lc <= tk
    streams = tuple((sub * tq, hh * HEAD_TILE) for sub in range(n_sub) for hh in range(2))

    def scores_into(s_ref, k2):
        maxes = []
        for st, (q0, lo) in enumerate(streams):
            s = _dot_nt(k2[:, lo:lo + HEAD_TILE], q_ref[0, q0:q0 + tq, lo:lo + HEAD_TILE])
            s_ref[st, 0:k2.shape[0], :] = s
            maxes.append(jnp.max(s, axis=0, keepdims=True))
        return tuple(maxes)

    def consume(carries, s_ref, n_keys, maxes, vt):
        out = []
        for hh in range(len(streams)):
            m, l, acc = carries[hh]
            m_new = jnp.maximum(m, maxes[hh])
            alpha = jnp.exp2(m - m_new)
            blocks, l_add = [], jnp.zeros_like(l)
            for r in range(0, n_keys, ATTN_KEY_BLOCK):
                p = jnp.exp2(s_ref[hh, r:r + ATTN_KEY_BLOCK, :] - m_new)
                l_add = l_add + jnp.sum(p, axis=0, keepdims=True)
                blocks.append(p.astype(BF16))
            acc = alpha * acc + _dot(vt, jnp.concatenate(blocks, axis=0))
            out.append((m_new, alpha * l + l_add, acc))
        return tuple(out)

    def keys(c):
        return kl_ref[0, pl.ds(pl.multiple_of(c * tk, tk), tk), :]

    def body(i, state):
        carries, max_a = state
        c = 2 * i
        max_b = scores_into(sb_ref, keys(c + 1))
        carries = consume(carries, sa_ref, tk, max_a, vl_ref[0, 0, c])
        max_a = scores_into(sa_ref, keys(c + 2))
        carries = consume(carries, sb_ref, tk, max_b, vl_ref[0, 0, c + 1])
        return carries, max_a

    init = (jnp.full((1, tq), -jnp.inf, F32), jnp.zeros((1, tq), F32), jnp.zeros((LANE, tq), F32))
    state = ((init,) * len(streams), scores_into(sa_ref, kl_ref[0, 0:tk, :]))
    carries, max_a = lax.fori_loop(0, (n_chunks - 2) // 2, body, state)
    max_b = scores_into(sb_ref, kl_ref[0, (n_chunks - 1) * tk:n_chunks * tk, :])
    carries = consume(carries, sa_ref, tk, max_a, vl_ref[0, 0, n_chunks - 2])
    max_c = scores_into(sa_ref, kc_ref[0])
    carries = consume(carries, sb_ref, tk, max_b, vl_ref[0, 0, n_chunks - 1])
    carries = consume(carries, sa_ref, lc, max_c, vc_ref[0, 0, 0])
    for sub in range(n_sub):
        (_, l_a, acc_a), (_, l_b, acc_b) = carries[2 * sub], carries[2 * sub + 1]
        out_t = jnp.concatenate([(acc_a / l_a)[:MLA_V], (acc_b / l_b)[MLA_V:]], axis=0)
        o_ref[0, sub * tq:(sub + 1) * tq, :] = out_t.T.astype(BF16)


def _attention(qa, ka, vt, ka_c, vt_c):
    b, s, _ = qa.shape
    lc = ka_c.shape[1]
    hp, n_chunks, _, tk = vt.shape[1:]
    tq = min(s, ATTN_TQ)
    tqb = min(s, ATTN_SUB * tq)
    n_streams = 2 * (tqb // tq)
    return pl.pallas_call(
        functools.partial(_attn_kernel, tk=tk, tq=tq),
        grid=(b, hp, s // tqb),
        in_specs=[pl.BlockSpec((1, tqb, 2 * HEAD_TILE), lambda bi, h, i: (bi, i, h)),
                  pl.BlockSpec((1, s, 2 * HEAD_TILE), lambda bi, h, i: (bi, 0, h)),
                  pl.BlockSpec((1, lc, 2 * HEAD_TILE), lambda bi, h, i: (bi, 0, h)),
                  pl.BlockSpec((1, 1, n_chunks, LANE, tk), lambda bi, h, i: (bi, h, 0, 0, 0)),
                  pl.BlockSpec((1, 1, 1, LANE, lc), lambda bi, h, i: (bi, h, 0, 0, 0))],
        out_specs=pl.BlockSpec((1, tqb, 2 * MLA_V), lambda bi, h, i: (bi, i, h)),
        out_shape=jax.ShapeDtypeStruct((b, s, MLA_HEADS * MLA_V), BF16),
        scratch_shapes=[pltpu.VMEM((n_streams, tk, tq), F32), pltpu.VMEM((n_streams, tk, tq), F32)],
        compiler_params=_cparams(("arbitrary", "arbitrary", "arbitrary")),
        name="attn",
    )(qa, ka, ka_c, vt, vt_c)


def _ret_kernel(q_ref, k_ref, v_ref, g_ref, kc_ref, vc_ref, eh_ref, el_ref, gret_ref, o_ref, rb_ref):
    c = RET_CHUNK
    n_lat = q_ref.shape[1] // c
    n_ctx = kc_ref.shape[1] // c

    def log_gamma(e):
        return jnp.log1p(-jnp.exp2(-e))

    lgf_a, lgf_b = log_gamma(eh_ref[0, 0:1, :]), log_gamma(eh_ref[0, 1:2, :])
    lgb_a, lgb_b = log_gamma(eh_ref[0, 2:3, :]), log_gamma(eh_ref[0, 3:4, :])
    lgf_l, lgb_l = log_gamma(el_ref[0, 0:1, :]), log_gamma(el_ref[0, 1:2, :])

    ri = lax.broadcasted_iota(jnp.int32, (c, c), 0)
    ci = lax.broadcasted_iota(jnp.int32, (c, c), 1)
    diff = (ri - ci).astype(F32)

    def decay(lgf, lgb):
        return jnp.where(diff >= 0, jnp.exp(lgf * diff), jnp.exp(-lgb * diff))

    d_a, d_b = decay(lgf_a, lgb_a), decay(lgf_b, lgb_b)
    pos = lax.broadcasted_iota(jnp.int32, (c, LANE), 0).astype(F32)
    xi_f = jnp.exp(lgf_l * (pos + 1.0))
    xi_b = jnp.exp(lgb_l * (c - pos))
    zeta_f = jnp.exp(lgf_l * (c - 1.0 - pos))
    zeta_b = jnp.exp(lgb_l * pos)
    hd = RET_DK // 2
    row_is_a = (ri // hd) % 2 == 0
    col_is_a = ci < RET_DV
    gc_f = jnp.exp(jnp.where(row_is_a, lgf_a, lgf_b) * c)
    gc_b = jnp.exp(jnp.where(row_is_a, lgb_a, lgb_b) * c)
    same_head = (row_is_a == col_is_a).astype(F32)
    lane = lax.broadcasted_iota(jnp.int32, (c, LANE), 1)
    qk_is_a = (lane // hd) % 2 == 0
    v_is_a = lane < RET_DV

    def state_update(r, gc, k, v, zeta):
        return gc * r + _dot_tn((k.astype(F32) * zeta).astype(BF16), v)

    def lat(ref, n):
        return ref[0, pl.ds(pl.multiple_of(n * c, c), c), :]

    zero = jnp.zeros((c, LANE), F32)

    r = zero
    for cc in reversed(range(n_ctx)):
        r = state_update(r, gc_b, kc_ref[0, cc * c:(cc + 1) * c, :], vc_ref[0, cc * c:(cc + 1) * c, :], zeta_b)

    def key_value_sum(k, v, zeta):
        return _dot_tn((k.astype(F32) * zeta).astype(BF16), v)

    def bwd_body(i, r):
        ns = [n_lat - 1 - (i * RET_UNROLL + u) for u in range(RET_UNROLL)]
        sums = [key_value_sum(lat(k_ref, n), lat(v_ref, n), zeta_b) for n in ns]
        for n, kv in zip(ns, sums):
            rb_ref[n] = r
            r = gc_b * r + kv
        return r

    assert n_lat % RET_UNROLL == 0
    lax.fori_loop(0, n_lat // RET_UNROLL, bwd_body, r)

    r = zero
    for cc in range(n_ctx):
        r = state_update(r, gc_f, kc_ref[0, cc * c:(cc + 1) * c, :], vc_ref[0, cc * c:(cc + 1) * c, :], zeta_f)

    gret = gret_ref[...]

    def fwd_body(i, r):
        ns = [i * RET_UNROLL + u for u in range(RET_UNROLL)]
        sums = [key_value_sum(lat(k_ref, n), lat(v_ref, n), zeta_f) for n in ns]
        scores = [chunk_scores(n) for n in ns]
        states = []
        for kv in sums:
            states.append(r)
            r = gc_f * r + kv
        ys = [chunk_mix(n, p2, rn) for n, p2, rn in zip(ns, scores, states)]
        for n, y in zip(ns, ys):
            chunk_store(n, y)
        return r

    def chunk_scores(n):
        q, k = lat(q_ref, n), lat(k_ref, n)
        qz = jnp.zeros_like(q)
        s_a = _dot_nt(jnp.where(qk_is_a, q, qz), k) * d_a
        s_b = _dot_nt(jnp.where(qk_is_a, qz, q), k) * d_b
        return jnp.concatenate([s_a, s_b], axis=1).astype(BF16)

    def chunk_mix(n, p2, r):
        q, v = lat(q_ref, n), lat(v_ref, n)
        vz = jnp.zeros_like(v)
        v2 = jnp.concatenate([jnp.where(v_is_a, v, vz), jnp.where(v_is_a, vz, v)], axis=0)
        qf = q.astype(F32)
        q2 = jnp.concatenate([(qf * xi_f).astype(BF16), (qf * xi_b).astype(BF16)], axis=1)
        r2 = jnp.concatenate([(r * same_head).astype(BF16), (rb_ref[n] * same_head).astype(BF16)], axis=0)
        return _dot(p2, v2) + _dot(q2, r2)

    def chunk_store(n, y):
        inv_n = 1.0 / RET_DV
        sum_a = jnp.sum(jnp.where(v_is_a, y, 0.0), axis=-1, keepdims=True)
        sum_b = jnp.sum(jnp.where(v_is_a, 0.0, y), axis=-1, keepdims=True)
        dlt = y - jnp.where(v_is_a, sum_a, sum_b) * inv_n
        sq = dlt * dlt
        var_a = jnp.sum(jnp.where(v_is_a, sq, 0.0), axis=-1, keepdims=True)
        var_b = jnp.sum(jnp.where(v_is_a, 0.0, sq), axis=-1, keepdims=True)
        yn = dlt * lax.rsqrt(jnp.where(v_is_a, var_a, var_b) * inv_n + EPS) * gret
        o_ref[0, pl.ds(pl.multiple_of(n * c, c), c), :] = (yn * _silu(lat(g_ref, n).astype(F32))).astype(BF16)

    lax.fori_loop(0, n_lat // RET_UNROLL, fwd_body, r)


def _retention(rq, rk, rv, rg, rk_c, rv_c, exp_f, exp_b, g_ret):
    b, s, n = rq.shape
    lc = rk_c.shape[1]
    n_t = n // LANE
    ef = exp_f.reshape(n_t, 2)
    eb = exp_b.reshape(n_t, 2)
    eh = jnp.broadcast_to(jnp.concatenate([ef, eb], axis=1)[:, :, None], (n_t, 4, LANE)).astype(F32)
    lane_is_b = (np.arange(LANE) // (RET_DK // 2)) % 2
    el = jnp.stack([ef[:, lane_is_b], eb[:, lane_is_b]], axis=1).astype(F32)
    tok = lambda length: pl.BlockSpec((1, length, LANE), lambda bi, t: (bi, 0, t))
    return pl.pallas_call(
        _ret_kernel,
        grid=(b, n_t),
        in_specs=[tok(s), tok(s), tok(s), tok(s), tok(lc), tok(lc),
                  pl.BlockSpec((1, 4, LANE), lambda bi, t: (t, 0, 0)),
                  pl.BlockSpec((1, 2, LANE), lambda bi, t: (t, 0, 0)),
                  pl.BlockSpec((1, LANE), lambda bi, t: (0, t))],
        out_specs=tok(s),
        out_shape=jax.ShapeDtypeStruct((b, s, n), BF16),
        scratch_shapes=[pltpu.VMEM((s // RET_CHUNK, RET_CHUNK, LANE), F32)],
        compiler_params=_cparams(("arbitrary", "arbitrary")),
        name="ret",
    )(rq, rk, rv, rg, rk_c, rv_c, eh, el, g_ret.reshape(1, n))


def _outproj_kernel(oa_ref, yr_ref, x_ref, mod_ref, wo_ref, gffn_ref, wr_ref, x1_ref, h2_ref, aff_ref):
    n_a = oa_ref.shape[2]
    y = _dot(oa_ref[0], wo_ref[0:n_a, :]) + _dot(yr_ref[0], wo_ref[n_a:, :])
    x1 = x_ref[0] + mod_ref[0, 2:3, :] * y
    x1_ref[0] = x1
    h2 = _rms(x1) * gffn_ref[...] * (1.0 + mod_ref[0, 4:5, :]) + mod_ref[0, 3:4, :]
    tm = h2.shape[0]
    for r in range(ROW_TILE):
        h2_ref[0, pl.ds(r, tm, stride=ROW_TILE), :] = h2[:, r * LANE:(r + 1) * LANE]
    logits =lax.dot_general(wr_ref[...], h2, (((1,), (1,)), ((), ())),
                             preferred_element_type=F32, precision=HIGHEST)
    ex = jnp.exp(logits - jnp.max(logits, axis=0, keepdims=True))
    aff_ref[0] = ex / jnp.sum(ex, axis=0, keepdims=True)


def _outproj(o_att, y_ret, x, mod, w_o, g_ffn, w_router_t):
    b, s, d = x.shape
    tm = min(s, 512)
    n_e = w_router_t.shape[0]
    tok = lambda n: pl.BlockSpec((1, tm, n), lambda bi, i: (bi, i, 0))
    full = lambda shape: pl.BlockSpec(shape, lambda bi, i: tuple(0 for _ in shape))
    return pl.pallas_call(
        _outproj_kernel,
        grid=(b, s // tm),
        in_specs=[tok(o_att.shape[2]), tok(y_ret.shape[2]), tok(d),
                  pl.BlockSpec((1, 6, d), lambda bi, i: (bi, 0, 0)),
                  full(w_o.shape), full((1, d)), full(w_router_t.shape)],
        out_specs=[tok(d), pl.BlockSpec((1, tm * ROW_TILE, LANE), lambda bi, i: (bi, i, 0)),
                   pl.BlockSpec((1, n_e, tm), lambda bi, i: (bi, 0, i))],
        out_shape=[jax.ShapeDtypeStruct((b, s, d), F32), jax.ShapeDtypeStruct((b, s * ROW_TILE, LANE), F32),
                   jax.ShapeDtypeStruct((b, n_e, s), F32)],
        compiler_params=_cparams(("arbitrary", "arbitrary")),
        name="outproj",
    )(o_att, y_ret, x, mod, w_o, g_ffn, w_router_t)


def _topk_kernel(aff_ref, idx_ref, c_ref, t_ref, *, cap):
    n_e, n_r, _ = c_ref.shape
    assert n_r <= LANE
    a = aff_ref[0]

    def count(mask):
        ones = jnp.where(mask, 1, 0)
        return jnp.sum(jnp.sum(ones, axis=1, keepdims=True), axis=2, keepdims=True)

    def as_float(bits):
        return pltpu.bitcast(bits, F32)

    def search(i, lo):
        cand = lo | jnp.left_shift(jnp.int32(1), 30 - i)
        return jnp.where(count(a >= as_float(cand)) >= cap, cand, lo)

    thr = lax.fori_loop(0, 31, search, jnp.zeros((n_e, 1, 1), jnp.int32))
    above = a >= as_float(thr + 1)
    need = cap - count(above)
    window = (a >= as_float(thr)) & jnp.logical_not(above)
    tok = (lax.broadcasted_iota(jnp.int32, (1, n_r, LANE), 1) * LANE
           + lax.broadcasted_iota(jnp.int32, (1, n_r, LANE), 2))

    def surplus(state):
        return jnp.max(state[1] - need) > 0

    def drop_one(state):
        win, n_win = state
        active = n_win > need
        inside = win > 0.5
        low = jnp.min(jnp.min(jnp.where(inside, a, jnp.inf), axis=1, keepdims=True), axis=2, keepdims=True)
        cand = inside & (a == low)
        last = jnp.max(jnp.max(jnp.where(cand, tok, -1), axis=1, keepdims=True), axis=2, keepdims=True)
        win = jnp.where(cand & (tok == last) & active, 0.0, win)
        return win, n_win - jnp.where(active, 1, 0)

    win, _ = lax.while_loop(surplus, drop_one, (jnp.where(window, 1.0, 0.0), count(window)))
    sel = above | (win > 0.5)

    li = lax.broadcasted_iota(jnp.int32, (LANE, LANE), 0)
    lj = lax.broadcasted_iota(jnp.int32, (LANE, LANE), 1)
    tri = jnp.where(li <= lj, 1.0, 0.0).astype(BF16)
    ones = jnp.ones((LANE, LANE), BF16)
    n_flat = n_e * n_r
    m2 = jnp.where(sel, 1.0, 0.0).astype(BF16).reshape(n_flat, LANE)
    within = _dot(m2, tri)
    tot = _dot(m2, ones)
    pi = lax.broadcasted_iota(jnp.int32, (n_flat, n_flat), 0)
    pj = lax.broadcasted_iota(jnp.int32, (n_flat, n_flat), 1)
    before = jnp.where((pi // n_r == pj // n_r) & (pj < pi), 1.0, 0.0).astype(BF16)
    off = _dot(before, tot.astype(BF16))
    c_ref[...] = (within + off).reshape(n_e, n_r, LANE)
    t_ref[...] = tot.reshape(n_e, n_r, LANE)

    incl = jnp.where(li <= lj, 1.0, 0.0).astype(BF16)[:n_r]
    jcol = lax.broadcasted_iota(jnp.int32, (cap, LANE), 0).astype(F32)
    lane = lax.broadcasted_iota(jnp.int32, (cap, LANE), 1)
    ones8 = jnp.ones((8, LANE), BF16)
    pad = jnp.zeros((LANE - n_r, LANE), F32)

    eye = jnp.where(li == lj, 1.0, 0.0).astype(BF16)[:n_r]

    def expert_group(i, _):
        es = [i * TOPK_GROUP + u for u in range(TOPK_GROUP)]
        tots = [t_ref[e].astype(BF16) for e in es]
        ends = [_dot_tn(t, incl)[0:1, :] for t in tots]
        starts = [end - _dot_tn(t, eye)[0:1, :] for t, end in zip(tots, ends)]
        in_rows = [(start <= jcol) & (end > jcol) & (lane < n_r) for start, end in zip(starts, ends)]
        c_pads = [jnp.concatenate([c_ref[e], pad], axis=0) if n_r < LANE else c_ref[e] for e in es]
        c_rows = [jnp.dot(jnp.where(m, 1.0, 0.0), c, preferred_element_type=F32, precision=HIGHEST)
                  for m, c in zip(in_rows, c_pads)]
        n_rows = [_dot_nt(ones8, jnp.where((end <= jcol) & (lane < n_r), 1.0, 0.0).astype(BF16)) for end in ends]
        n_lanes = [_dot_nt(ones8, jnp.where(c <= jcol, 1.0, 0.0).astype(BF16)) for c in c_rows]
        for e, nr, nl in zip(es, n_rows, n_lanes):
            idx_ref[0, pl.ds(e, 1), :] = (nr[0:1, :] * LANE + nl[0:1, :]).astype(jnp.int32)
        return 0

    assert n_e % TOPK_GROUP == 0
    lax.fori_loop(0, n_e // TOPK_GROUP, expert_group, 0)


def _topk(aff, cap):
    b, n_e, s = aff.shape
    n_r = s // LANE
    aff4 = aff.reshape(b, n_e, n_r, LANE)
    return pl.pallas_call(
        functools.partial(_topk_kernel, cap=cap),
        grid=(b,),
        in_specs=[pl.BlockSpec((1, n_e, n_r, LANE), lambda bi: (bi, 0, 0, 0))],
        out_specs=pl.BlockSpec((1, n_e, cap), lambda bi: (bi, 0, 0)),
        out_shape=jax.ShapeDtypeStruct((b, n_e, cap), jnp.int32),
        scratch_shapes=[pltpu.VMEM((n_e, n_r, LANE), F32), pltpu.VMEM((n_e, n_r, LANE), F32)],
        compiler_params=_cparams(("arbitrary",)),
        name="topk",
    )(aff4)


def _ffn_kernel(idx_ref, aff_ref, h2_hbm, wg_ref, wu_ref, wd_ref, out_hbm,
                h2_v, acc_v, sem, *part_bufs, tm):
    bi = pl.program_id(0)
    e = pl.program_id(1)
    cap = idx_ref.shape[2]
    n_parts = cap // tm
    xs_bufs, y_bufs = part_bufs[:n_parts], part_bufs[n_parts:]

    @pl.when(e == 0)
    def _():
        cp = pltpu.make_async_copy(h2_hbm.at[bi], h2_v, sem.at[0])
        cp.start()
        acc_v[...] = jnp.zeros_like(acc_v)
        cp.wait()

    def tile(t):
        return pl.ds(pl.multiple_of(t * ROW_TILE, ROW_TILE), ROW_TILE)

    def gather(part):
        for j in range(tm):
            xs_bufs[part][j * ROW_TILE:(j + 1) * ROW_TILE, :] = h2_v[tile(idx_ref[0, 0, part * tm + j]), :]

    def experts(part):
        xb = jnp.concatenate([xs_bufs[part][pl.ds(r, tm, stride=ROW_TILE), :] for r in range(ROW_TILE)],
                             axis=1).astype(BF16)
        a = _dot(xb, wg_ref[0])
        u = _dot(xb, wu_ref[0])
        y = _dot((_silu(a) * u).astype(BF16), wd_ref[0])
        for r in range(ROW_TILE):
            y_bufs[part][pl.ds(r, tm, stride=ROW_TILE), :] = y[:, r * LANE:(r + 1) * LANE]

    def scatter(part):
        for j0 in range(0, tm, FFN_UNROLL):
            toks = [idx_ref[0, 0, part * tm + j0 + u] for u in range(FFN_UNROLL)]
            rows = [acc_v[tile(t), :]
                    + y_bufs[part][(j0 + u) * ROW_TILE:(j0 + u + 1) * ROW_TILE, :] * aff_ref[0, 0, t]
                    for u, t in enumerate(toks)]
            for t, row in zip(toks, rows):
                acc_v[tile(t), :] = row

    gather(0)
    for part in range(n_parts):
        experts(part)
        if part + 1 < n_parts:
            gather(part + 1)
        scatter(part)

    @pl.when(e == pl.num_programs(1) - 1)
    def _():
        cp = pltpu.make_async_copy(acc_v, out_hbm.at[bi], sem.at[1])
        cp.start()
        cp.wait()


def _ffn(idx, aff, h2, w_gate, w_up, w_down):
    b, rows, _ = h2.shape
    s = rows // ROW_TILE
    n_e, d, f = w_gate.shape
    cap = idx.shape[2]
    tm = min(cap, 256)
    assert d == ROW_TILE * LANE and tm % FFN_UNROLL == 0
    smem = lambda n: pl.BlockSpec((1, 1, n), lambda bi, e: (bi * n_e + e, 0, 0), memory_space=pltpu.SMEM)
    idx = idx.reshape(b * n_e, 1, cap)
    aff = aff.reshape(b * n_e, 1, s)
    return pl.pallas_call(
        functools.partial(_ffn_kernel, tm=tm),
        grid=(b, n_e),
        in_specs=[smem(cap), smem(s),
                  pl.BlockSpec(memory_space=pl.ANY),
                  pl.BlockSpec((1, d, f), lambda bi, e: (e, 0, 0)),
                  pl.BlockSpec((1, d, f), lambda bi, e: (e, 0, 0)),
                  pl.BlockSpec((1, f, d), lambda bi, e: (e, 0, 0))],
        out_specs=pl.BlockSpec(memory_space=pl.ANY),
        out_shape=jax.ShapeDtypeStruct((b, rows, LANE), F32),
        scratch_shapes=[pltpu.VMEM((rows, LANE), F32), pltpu.VMEM((rows, LANE), F32),
                        pltpu.SemaphoreType.DMA((2,))]
        + [pltpu.VMEM((tm * ROW_TILE, LANE), F32)] * (2 * (cap // tm)),
        compiler_params=_cparams(("arbitrary", "arbitrary"), FFN_VMEM_LIMIT),
        name="ffn",
    )(idx, aff, h2, w_gate, w_up, w_down)


def _final_kernel(x1_ref, acc_ref, mod_ref, g_ref, o_ref):
    tm = x1_ref.shape[1]
    acc = jnp.concatenate([acc_ref[0, pl.ds(r, tm, stride=ROW_TILE), :] for r in range(ROW_TILE)], axis=1)
    x2 = x1_ref[0] + mod_ref[0, 5:6, :] * acc
    o_ref[0] = _rms(x2) * g_ref[...]


def _final(x1, acc, mod, g_final):
    b, s, d = x1.shape
    tm = min(s, 512)
    tok = pl.BlockSpec((1, tm, d), lambda bi, i: (bi, i, 0))
    return pl.pallas_call(
        _final_kernel,
        grid=(b, s // tm),
        in_specs=[tok, pl.BlockSpec((1, tm * ROW_TILE, LANE), lambda bi, i: (bi, i, 0)),
                  pl.BlockSpec((1, 6, d), lambda bi, i: (bi, 0, 0)),
                  pl.BlockSpec((1, d), lambda bi, i: (0, 0))],
        out_specs=tok,
        out_shape=jax.ShapeDtypeStruct((b, s, d), F32),
        compiler_params=_cparams(("arbitrary", "arbitrary")),
        name="final",
    )(x1, acc, mod, g_final.reshape(1, d))


def kernel(x, c, ctx, c_ctx, w_ada, b_ada, g_norm_mix, g_norm_ffn, w_in, g_q_lora, g_kv_lora, w_uq, w_ukv,
           ret_exp_fwd, ret_exp_bwd, g_ret, w_o, w_router, w_exp_gate, w_exp_up, w_exp_down, g_final):
    b, s, d = x.shape
    depth = w_ada.shape[0]
    assert depth == 1, "the context stream update is only needed for depth > 1"
    assert s % LANE == 0 and s % GRID_W == 0
    cap = EC_CAPACITY * s // N_EXPERTS

    rows = -(-(b + 1) // 8) * 8
    cc = jnp.zeros((rows, d), F32).at[:b].set(c).at[b].set(c_ctx)
    mods = _ada(cc, w_ada[0], b_ada[0])
    mod = mods[:b].reshape(b, 6, d)
    mod_c = mods[b:b + 1].reshape(1, 6, d)

    w_in_ext, w_uq_ext, w_ukv_ext = _prep_weights(w_in[0], w_uq[0], w_ukv[0])
    tabs = _rope_tables(s)
    gmix = g_norm_mix[0].reshape(1, d)
    gq = g_q_lora[0].reshape(1, -1)
    gkv = g_kv_lora[0].reshape(1, -1)

    qa, ka, va, rq, rk, rv, rg = _inproj(x, mod, gmix, w_in_ext, gq, gkv, w_uq_ext, w_ukv_ext, tabs, is_ctx=False)
    ka_c, va_c, rk_c, rv_c = _inproj(ctx, mod_c, gmix, w_in_ext, gq, gkv, w_uq_ext, w_ukv_ext, None, is_ctx=True)

    o_att = _attention(qa, ka, va, ka_c, va_c)
    y_ret = _retention(rq, rk, rv, rg, rk_c, rv_c, ret_exp_fwd[0], ret_exp_bwd[0], g_ret[0])

    x1, h2, aff = _outproj(o_att, y_ret, x, mod, w_o[0].astype(BF16), g_norm_ffn[0].reshape(1, d),
                           w_router[0].T)
    idx = _topk(aff, cap)
    acc = _ffn(idx, aff, h2, w_exp_gate[0].astype(BF16), w_exp_up[0].astype(BF16), w_exp_down[0].astype(BF16))
    return _final(x1, acc, mod, g_final)
```

```python
import functools

import numpy as np
import jax
import jax.numpy as jnp
from jax import lax
from jax.experimental import pallas as pl
from jax.experimental.pallas import tpu as pltpu

GRID_W = 64
MLA_HEADS = 8
MLA_Q_RANK = 256
MLA_KV_RANK = 128
MLA_NOPE = 64
MLA_ROPE = 32
MLA_V = 64
RET_HEADS = 8
RET_DK = 64
RET_DV = 64
RET_CHUNK = 128
N_EXPERTS = 16
EC_CAPACITY = 2
ROPE_BASE = 10000.0
EPS = 1e-6

LANE = 128
HEAD_TILE = 128
ATTN_TQ = 256
ATTN_SUB = 8
ATTN_KEY_BLOCK = 64
OUTPROJ_SPLIT = 2
TOPK_GROUP = 4
RET_UNROLL = 8
ROW_TILE = 8
FFN_UNROLL = 8
VMEM_LIMIT = 48 * 1024 * 1024
FFN_VMEM_LIMIT = 60 * 1024 * 1024
LOG2E = 1.4426950408889634

F32 = jnp.float32
BF16 = jnp.bfloat16
HIGHEST = lax.Precision.HIGHEST

W_CQ = (0, 256)
W_CKV = (256, 384)
W_KR = (384, 512)
W_RQ = (512, 1024)
W_RK = (1024, 1536)
W_RV = (1536, 2048)
W_RG = (2048, 2560)
D_IN_EXT = 2560


def _cparams(sem, limit=VMEM_LIMIT):
    return pltpu.CompilerParams(dimension_semantics=sem, vmem_limit_bytes=limit)


def _dot(a, b):
    return jnp.dot(a, b, preferred_element_type=F32)


def _dot_nt(a, b):
    return lax.dot_general(a, b, (((1,), (1,)), ((), ())), preferred_element_type=F32)


def _dot_tn(a, b):
    return lax.dot_general(a, b, (((0,), (0,)), ((), ())), preferred_element_type=F32)


def _silu(x):
    return x * jax.nn.sigmoid(x)


def _mla_lane_maps():
    half = MLA_ROPE // 2
    src = np.zeros(HEAD_TILE, np.int32)
    valid = np.zeros(HEAD_TILE, bool)
    src[0:half] = MLA_NOPE + np.arange(half)
    valid[0:half] = True
    src[half:64] = np.arange(64 - half)
    valid[half:64] = True
    src[64:64 + half] = MLA_NOPE + half + np.arange(half)
    valid[64:64 + half] = True
    n_rest = MLA_NOPE - (64 - half)
    src[64 + half:64 + half + n_rest] = (64 - half) + np.arange(n_rest)
    valid[64 + half:64 + half + n_rest] = True
    return src, valid


def _prep_weights(w_in, w_uq, w_ukv):
    d_model = w_in.shape[0]
    half = MLA_ROPE // 2
    src, valid = _mla_lane_maps()
    is_rope = valid & (src >= MLA_NOPE)
    is_nope = valid & (src < MLA_NOPE)

    o = 0
    cq = w_in[:, o:o + MLA_Q_RANK]; o += MLA_Q_RANK
    ckv = w_in[:, o:o + MLA_KV_RANK]; o += MLA_KV_RANK
    kr = w_in[:, o:o + MLA_ROPE]; o += MLA_ROPE
    n_r = RET_HEADS * RET_DK
    rq = w_in[:, o:o + n_r]; o += n_r
    rk = w_in[:, o:o + n_r]; o += n_r
    rv = w_in[:, o:o + RET_HEADS * RET_DV]; o += RET_HEADS * RET_DV
    rg = w_in[:, o:o + RET_HEADS * RET_DV]

    kr_idx = np.where(is_rope, src - MLA_NOPE, 0)
    kr_ext = jnp.where(jnp.asarray(is_rope)[None, :], kr[:, kr_idx], 0.0)

    hd = RET_DK // 2
    lanes = np.arange(n_r)
    tile, l = lanes // LANE, lanes % LANE
    grp = l // hd
    head = 2 * tile + (grp % 2)
    perm = head * RET_DK + (grp // 2) * hd + (l % hd)
    rq_p = rq[:, perm]
    rk_p = rk[:, perm] * (RET_DK ** -0.5)

    w_in_ext = jnp.concatenate([cq, ckv, kr_ext, rq_p, rk_p, rv, rg], axis=1).astype(BF16)
    assert w_in_ext.shape == (d_model, D_IN_EXT)

    dq = MLA_NOPE + MLA_ROPE
    q_cols = (np.arange(MLA_HEADS)[:, None] * dq + src[None, :]).reshape(-1)
    q_valid = np.tile(valid, MLA_HEADS)
    w_uq_ext = jnp.where(jnp.asarray(q_valid)[None, :], w_uq[:, q_cols], 0.0).astype(BF16)

    dkv = MLA_NOPE + MLA_V
    k_cols = (np.arange(MLA_HEADS)[:, None] * dkv + np.where(is_nope, src, 0)[None, :]).reshape(-1)
    k_valid = np.tile(is_nope, MLA_HEADS)
    w_uk_ext = jnp.where(jnp.asarray(k_valid)[None, :], w_ukv[:, k_cols], 0.0)
    v_cols = (np.arange(MLA_HEADS)[:, None] * dkv + MLA_NOPE + np.arange(MLA_V)[None, :]).reshape(-1)
    w_uv = w_ukv[:, v_cols]
    w_ukv_ext = jnp.concatenate([w_uk_ext, w_uv], axis=1).astype(BF16)
    return w_in_ext, w_uq_ext, w_ukv_ext


def _rope_patterns():
    pat = np.zeros((8, LANE), np.float32)
    half = MLA_ROPE // 2
    nf = MLA_ROPE // 4
    inv = ROPE_BASE ** (-np.arange(nf, dtype=np.float64) / nf)
    for base, sign in ((0, -1.0), (64, 1.0)):
        for i in range(half):
            pat[0, base + i] = inv[i % nf]
            pat[1, base + i] = 1.0 if i < nf else 0.0
            pat[2, base + i] = sign
    hd = RET_DK // 2
    nf = RET_DK // 4
    inv = ROPE_BASE ** (-np.arange(nf, dtype=np.float64) / nf)
    for l in range(LANE):
        i = l % hd
        pat[3, l] = inv[i % nf]
        pat[4, l] = 1.0 if i < nf else 0.0
        pat[5, l] = -1.0 if l < 64 else 1.0
    return jnp.asarray(pat)


def _ada_kernel(c_ref, w_ref, b_ref, o_ref):
    s = _silu(c_ref[...])
    o_ref[...] = jnp.dot(s, w_ref[...], preferred_element_type=F32, precision=HIGHEST) + b_ref[...]


def _ada(cc, w_ada, b_ada):
    rows, d = cc.shape
    n = w_ada.shape[1]
    tn = 1024
    return pl.pallas_call(
        _ada_kernel,
        grid=(n // tn,),
        in_specs=[pl.BlockSpec((rows, d), lambda j: (0, 0)),
                  pl.BlockSpec((d, tn), lambda j: (0, j)),
                  pl.BlockSpec((1, tn), lambda j: (0, j))],
        out_specs=pl.BlockSpec((rows, tn), lambda j: (0, j)),
        out_shape=jax.ShapeDtypeStruct((rows, n), F32),
        compiler_params=_cparams(("arbitrary",)),
        name="ada",
    )(cc, w_ada, b_ada.reshape(1, n))


def _rope_kernel(pat_ref, o_ref, *, tm):
    t = pl.program_id(0) * tm + lax.broadcasted_iota(jnp.int32, (tm, LANE), 0)
    row = (t // GRID_W).astype(F32)
    col = (t % GRID_W).astype(F32)
    for k in range(2):
        inv = pat_ref[3 * k:3 * k + 1, :]
        use_row = pat_ref[3 * k + 1:3 * k + 2, :]
        sign = pat_ref[3 * k + 2:3 * k + 3, :]
        ang = jnp.where(use_row > 0.5, row, col) * inv
        active = sign != 0.0
        o_ref[2 * k] = jnp.where(active, jnp.cos(ang), 1.0)
        o_ref[2 * k + 1] = jnp.where(active, sign * jnp.sin(ang), 0.0)


def _rope_tables(seq):
    tm = min(seq, 512)
    return pl.pallas_call(
        functools.partial(_rope_kernel, tm=tm),
        grid=(seq // tm,),
        in_specs=[pl.BlockSpec((8, LANE), lambda i: (0, 0))],
        out_specs=pl.BlockSpec((4, tm, LANE), lambda i: (0, i, 0)),
        out_shape=jax.ShapeDtypeStruct((4, seq, LANE), F32),
        compiler_params=_cparams(("arbitrary",)),
        name="rope",
    )(_rope_patterns())


def _rms(x):
    return x * lax.rsqrt(jnp.mean(x * x, axis=-1, keepdims=True) + EPS)


def _rot(x, cos, sin):
    return x * cos + pltpu.roll(x, 64, axis=1) * sin


def _inproj_kernel(*refs, is_ctx):
    if is_ctx:
        (x_ref, mod_ref, gmix_ref, win_ref, gkv_ref, wukv_ref,
         ka_ref, va_ref, rk_ref, rv_ref) = refs
    else:
        (x_ref, mod_ref, gmix_ref, win_ref, gq_ref, gkv_ref, wuq_ref, wukv_ref, tab_ref,
         qa_ref, ka_ref, va_ref, rq_ref, rk_ref, rv_ref, rg_ref) = refs
    n_k = MLA_HEADS * HEAD_TILE
    h = _rms(x_ref[0]) * gmix_ref[...] * (1.0 + mod_ref[0, 1:2, :]) + mod_ref[0, 0:1, :]
    z = _dot(h.astype(BF16), win_ref[...])
    ckv = _rms(z[:, W_CKV[0]:W_CKV[1]]) * gkv_ref[...]
    kv = _dot(ckv.astype(BF16), wukv_ref[...])
    kr = z[:, W_KR[0]:W_KR[1]]
    if not is_ctx:
        cm, sm, cr, sr = tab_ref[0], tab_ref[1], tab_ref[2], tab_ref[3]
        kr = _rot(kr, cm, sm)
    for hh in range(MLA_HEADS):
        ka_ref[0, :, hh * HEAD_TILE:(hh + 1) * HEAD_TILE] = (
            kv[:, hh * HEAD_TILE:(hh + 1) * HEAD_TILE] + kr).astype(BF16)
    for hp in range(MLA_HEADS // 2):
        va_ref[0, hp, 0] = kv[:, n_k + hp * LANE:n_k + (hp + 1) * LANE].T.astype(BF16)
    rv_ref[0] = z[:, W_RV[0]:W_RV[1]].astype(BF16)
    n_t = (W_RK[1] - W_RK[0]) // LANE
    if is_ctx:
        rk_ref[0] = z[:, W_RK[0]:W_RK[1]].astype(BF16)
        return
    for t in range(n_t):
        rk_ref[0, :, t * LANE:(t + 1) * LANE] = _rot(
            z[:, W_RK[0] + t * LANE:W_RK[0] + (t + 1) * LANE], cr, sr).astype(BF16)
        rq_ref[0, :, t * LANE:(t + 1) * LANE] = _rot(
            z[:, W_RQ[0] + t * LANE:W_RQ[0] + (t + 1) * LANE], cr, sr).astype(BF16)
    rg_ref[0] = z[:, W_RG[0]:W_RG[1]].astype(BF16)
    cq = _rms(z[:, W_CQ[0]:W_CQ[1]]) * gq_ref[...]
    q = _dot(cq.astype(BF16), wuq_ref[...])
    qscale = (MLA_NOPE + MLA_ROPE) ** -0.5 * LOG2E
    for hh in range(MLA_HEADS):
        qh = _rot(q[:, hh * HEAD_TILE:(hh + 1) * HEAD_TILE], cm, sm)
        qa_ref[0, :, hh * HEAD_TILE:(hh + 1) * HEAD_TILE] = (qh * qscale).astype(BF16)


def _inproj(x, mod, gmix, w_in_ext, gq, gkv, w_uq_ext, w_ukv_ext, tabs, *, is_ctx):
    b, s, d = x.shape
    tm = min(s, 512)
    n_k = MLA_HEADS * HEAD_TILE
    n_r = RET_HEADS * RET_DK
    hp = MLA_HEADS // 2
    per_b = mod.shape[0] > 1
    full = lambda shape: pl.BlockSpec(shape, lambda bi, i: tuple(0 for _ in shape))
    tok = lambda n: pl.BlockSpec((1, tm, n), lambda bi, i: (bi, i, 0))
    tok_shape = lambda n: jax.ShapeDtypeStruct((b, s, n), BF16)
    mod_spec = pl.BlockSpec((1, 6, d), (lambda bi, i: (bi, 0, 0)) if per_b else (lambda bi, i: (0, 0, 0)))
    vt_spec = pl.BlockSpec((1, hp, 1, LANE, tm), lambda bi, i: (bi, 0, i, 0, 0))
    vt_shape = jax.ShapeDtypeStruct((b, hp, s // tm, LANE, tm), BF16)
    if is_ctx:
        ins = [x, mod, gmix, w_in_ext, gkv, w_ukv_ext]
        in_specs = [tok(d), mod_spec, full((1, d)), full(w_in_ext.shape), full((1, MLA_KV_RANK)),
                    full(w_ukv_ext.shape)]
        outs = [(tok(n_k), tok_shape(n_k)), (vt_spec, vt_shape)] + [(tok(n_r), tok_shape(n_r))] * 2
    else:
        ins = [x, mod, gmix, w_in_ext, gq, gkv, w_uq_ext, w_ukv_ext, tabs]
        in_specs = [tok(d), mod_spec, full((1, d)), full(w_in_ext.shape), full((1, MLA_Q_RANK)),
                    full((1, MLA_KV_RANK)), full(w_uq_ext.shape), full(w_ukv_ext.shape),
                    pl.BlockSpec((4, tm, LANE), lambda bi, i: (0, i, 0))]
        outs = [(tok(n_k), tok_shape(n_k))] * 2 + [(vt_spec, vt_shape)] + [(tok(n_r), tok_shape(n_r))] * 4
    return pl.pallas_call(
        functools.partial(_inproj_kernel, is_ctx=is_ctx),
        grid=(b, s // tm),
        in_specs=in_specs,
        out_specs=[o[0] for o in outs],
        out_shape=[o[1] for o in outs],
        compiler_params=_cparams(("arbitrary", "arbitrary")),
        name="inproj_ctx" if is_ctx else "inproj",
    )(*ins)


def _attn_kernel(q_ref, kl_ref, kc_ref, vl_ref, vc_ref, o_ref, sa_ref, sb_ref, *, tk, tq):
    n_sub = q_ref.shape[1] // tq
    n_chunks = kl_ref.shape[1] // tk
    lc = kc_ref.shape[1]
    assert n_chunks >= 2 and n_chunks % 2 == 0 and lc <= tk
    streams = tuple((sub * tq, hh * HEAD_TILE) for sub in range(n_sub) for hh in range(2))

    def scores_into(s_ref, k2):
        maxes = []
        for st, (q0, lo) in enumerate(streams):
            s = _dot_nt(k2[:, lo:lo + HEAD_TILE], q_ref[0, q0:q0 + tq, lo:lo + HEAD_TILE])
            s_ref[st, 0:k2.shape[0], :] = s
            maxes.append(jnp.max(s, axis=0, keepdims=True))
        return tuple(maxes)

    def consume(carries, s_ref, n_keys, maxes, vt):
        out = []
        for hh in range(len(streams)):
            m, l, acc = carries[hh]
            m_new = jnp.maximum(m, maxes[hh])
            alpha = jnp.exp2(m - m_new)
            blocks, l_add = [], jnp.zeros_like(l)
            for r in range(0, n_keys, ATTN_KEY_BLOCK):
                p = jnp.exp2(s_ref[hh, r:r + ATTN_KEY_BLOCK, :] - m_new)
                l_add = l_add + jnp.sum(p, axis=0, keepdims=True)
                blocks.append(p.astype(BF16))
            acc = alpha * acc + _dot(vt, jnp.concatenate(blocks, axis=0))
            out.append((m_new, alpha * l + l_add, acc))
        return tuple(out)

    def keys(c):
        return kl_ref[0, pl.ds(pl.multiple_of(c * tk, tk), tk), :]

    def body(i, state):
        carries, max_a = state
        c = 2 * i
        max_b = scores_into(sb_ref, keys(c + 1))
        carries = consume(carries, sa_ref, tk, max_a, vl_ref[0, 0, c])
        max_a = scores_into(sa_ref, keys(c + 2))
        carries = consume(carries, sb_ref, tk, max_b, vl_ref[0, 0, c + 1])
        return carries, max_a

    init = (jnp.full((1, tq), -jnp.inf, F32), jnp.zeros((1, tq), F32), jnp.zeros((LANE, tq), F32))
    state = ((init,) * len(streams), scores_into(sa_ref, kl_ref[0, 0:tk, :]))
    carries, max_a = lax.fori_loop(0, (n_chunks - 2) // 2, body, state)
    max_b = scores_into(sb_ref, kl_ref[0, (n_chunks - 1) * tk:n_chunks * tk, :])
    carries = consume(carries, sa_ref, tk, max_a, vl_ref[0, 0, n_chunks - 2])
    max_c = scores_into(sa_ref, kc_ref[0])
    carries = consume(carries, sb_ref, tk, max_b, vl_ref[0, 0, n_chunks - 1])
    carries = consume(carries, sa_ref, lc, max_c, vc_ref[0, 0, 0])
    for sub in range(n_sub):
        (_, l_a, acc_a), (_, l_b, acc_b) = carries[2 * sub], carries[2 * sub + 1]
        out_t = jnp.concatenate([(acc_a / l_a)[:MLA_V], (acc_b / l_b)[MLA_V:]], axis=0)
        o_ref[0, sub * tq:(sub + 1) * tq, :] = out_t.T.astype(BF16)


def _attention(qa, ka, vt, ka_c, vt_c):
    b, s, _ = qa.shape
    lc = ka_c.shape[1]
    hp, n_chunks, _, tk = vt.shape[1:]
    tq = min(s, ATTN_TQ)
    tqb = min(s, ATTN_SUB * tq)
    n_streams = 2 * (tqb // tq)
    return pl.pallas_call(
        functools.partial(_attn_kernel, tk=tk, tq=tq),
        grid=(b, hp, s // tqb),
        in_specs=[pl.BlockSpec((1, tqb, 2 * HEAD_TILE), lambda bi, h, i: (bi, i, h)),
                  pl.BlockSpec((1, s, 2 * HEAD_TILE), lambda bi, h, i: (bi, 0, h)),
                  pl.BlockSpec((1, lc, 2 * HEAD_TILE), lambda bi, h, i: (bi, 0, h)),
                  pl.BlockSpec((1, 1, n_chunks, LANE, tk), lambda bi, h, i: (bi, h, 0, 0, 0)),
                  pl.BlockSpec((1, 1, 1, LANE, lc), lambda bi, h, i: (bi, h, 0, 0, 0))],
        out_specs=pl.BlockSpec((1, tqb, 2 * MLA_V), lambda bi, h, i: (bi, i, h)),
        out_shape=jax.ShapeDtypeStruct((b, s, MLA_HEADS * MLA_V), BF16),
        scratch_shapes=[pltpu.VMEM((n_streams, tk, tq), F32), pltpu.VMEM((n_streams, tk, tq), F32)],
        compiler_params=_cparams(("arbitrary", "arbitrary", "arbitrary")),
        name="attn",
    )(qa, ka, ka_c, vt, vt_c)


def _ret_kernel(q_ref, k_ref, v_ref, g_ref, kc_ref, vc_ref, eh_ref, el_ref, gret_ref, o_ref, rb_ref):
    c = RET_CHUNK
    n_lat = q_ref.shape[1] // c
    n_ctx = kc_ref.shape[1] // c

    def log_gamma(e):
        return jnp.log1p(-jnp.exp2(-e))

    lgf_a, lgf_b = log_gamma(eh_ref[0, 0:1, :]), log_gamma(eh_ref[0, 1:2, :])
    lgb_a, lgb_b = log_gamma(eh_ref[0, 2:3, :]), log_gamma(eh_ref[0, 3:4, :])
    lgf_l, lgb_l = log_gamma(el_ref[0, 0:1, :]), log_gamma(el_ref[0, 1:2, :])

    ri = lax.broadcasted_iota(jnp.int32, (c, c), 0)
    ci = lax.broadcasted_iota(jnp.int32, (c, c), 1)
    diff = (ri - ci).astype(F32)

    def decay(lgf, lgb):
        return jnp.where(diff >= 0, jnp.exp(lgf * diff), jnp.exp(-lgb * diff))

    d_a, d_b = decay(lgf_a, lgb_a), decay(lgf_b, lgb_b)
    pos = lax.broadcasted_iota(jnp.int32, (c, LANE), 0).astype(F32)
    xi_f = jnp.exp(lgf_l * (pos + 1.0))
    xi_b = jnp.exp(lgb_l * (c - pos))
    zeta_f = jnp.exp(lgf_l * (c - 1.0 - pos))
    zeta_b = jnp.exp(lgb_l * pos)
    hd = RET_DK // 2
    row_is_a = (ri // hd) % 2 == 0
    col_is_a = ci < RET_DV
    gc_f = jnp.exp(jnp.where(row_is_a, lgf_a, lgf_b) * c)
    gc_b = jnp.exp(jnp.where(row_is_a, lgb_a, lgb_b) * c)
    same_head = (row_is_a == col_is_a).astype(F32)
    lane = lax.broadcasted_iota(jnp.int32, (c, LANE), 1)
    qk_is_a = (lane // hd) % 2 == 0
    v_is_a = lane < RET_DV

    def state_update(r, gc, k, v, zeta):
        return gc * r + _dot_tn((k.astype(F32) * zeta).astype(BF16), v)

    def lat(ref, n):
        return ref[0, pl.ds(pl.multiple_of(n * c, c), c), :]

    zero = jnp.zeros((c, LANE), F32)

    r = zero
    for cc in reversed(range(n_ctx)):
        r = state_update(r, gc_b, kc_ref[0, cc * c:(cc + 1) * c, :], vc_ref[0, cc * c:(cc + 1) * c, :], zeta_b)

    def key_value_sum(k, v, zeta):
        return _dot_tn((k.astype(F32) * zeta).astype(BF16), v)

    def bwd_body(i, r):
        ns = [n_lat - 1 - (i * RET_UNROLL + u) for u in range(RET_UNROLL)]
        sums = [key_value_sum(lat(k_ref, n), lat(v_ref, n), zeta_b) for n in ns]
        for n, kv in zip(ns, sums):
            rb_ref[n] = r
            r = gc_b * r + kv
        return r

    assert n_lat % RET_UNROLL == 0
    lax.fori_loop(0, n_lat // RET_UNROLL, bwd_body, r)

    r = zero
    for cc in range(n_ctx):
        r = state_update(r, gc_f, kc_ref[0, cc * c:(cc + 1) * c, :], vc_ref[0, cc * c:(cc + 1) * c, :], zeta_f)

    gret = gret_ref[...]

    def fwd_body(i, r):
        ns = [i * RET_UNROLL + u for u in range(RET_UNROLL)]
        sums = [key_value_sum(lat(k_ref, n), lat(v_ref, n), zeta_f) for n in ns]
        scores = [chunk_scores(n) for n in ns]
        states = []
        for kv in sums:
            states.append(r)
            r = gc_f * r + kv
        ys = [chunk_mix(n, p2, rn) for n, p2, rn in zip(ns, scores, states)]
        for n, y in zip(ns, ys):
            chunk_store(n, y)
        return r

    def chunk_scores(n):
        q, k = lat(q_ref, n), lat(k_ref, n)
        qz = jnp.zeros_like(q)
        s_a = _dot_nt(jnp.where(qk_is_a, q, qz), k) * d_a
        s_b = _dot_nt(jnp.where(qk_is_a, qz, q), k) * d_b
        return jnp.concatenate([s_a, s_b], axis=1).astype(BF16)

    def chunk_mix(n, p2, r):
        q, v = lat(q_ref, n), lat(v_ref, n)
        vz = jnp.zeros_like(v)
        v2 = jnp.concatenate([jnp.where(v_is_a, v, vz), jnp.where(v_is_a, vz, v)], axis=0)
        qf = q.astype(F32)
        q2 = jnp.concatenate([(qf * xi_f).astype(BF16), (qf * xi_b).astype(BF16)], axis=1)
        r2 = jnp.concatenate([(r * same_head).astype(BF16), (rb_ref[n] * same_head).astype(BF16)], axis=0)
        return _dot(p2, v2) + _dot(q2, r2)

    def chunk_store(n, y):
        inv_n = 1.0 / RET_DV
        sum_a = jnp.sum(jnp.where(v_is_a, y, 0.0), axis=-1, keepdims=True)
        sum_b = jnp.sum(jnp.where(v_is_a, 0.0, y), axis=-1, keepdims=True)
        dlt = y - jnp.where(v_is_a, sum_a, sum_b) * inv_n
        sq = dlt * dlt
        var_a = jnp.sum(jnp.where(v_is_a, sq, 0.0), axis=-1, keepdims=True)
        var_b = jnp.sum(jnp.where(v_is_a, 0.0, sq), axis=-1, keepdims=True)
        yn = dlt * lax.rsqrt(jnp.where(v_is_a, var_a, var_b) * inv_n + EPS) * gret
        o_ref[0, pl.ds(pl.multiple_of(n * c, c), c), :] = (yn * _silu(lat(g_ref, n).astype(F32))).astype(BF16)

    lax.fori_loop(0, n_lat // RET_UNROLL, fwd_body, r)


def _retention(rq, rk, rv, rg, rk_c, rv_c, exp_f, exp_b, g_ret):
    b, s, n = rq.shape
    lc = rk_c.shape[1]
    n_t = n // LANE
    ef = exp_f.reshape(n_t, 2)
    eb = exp_b.reshape(n_t, 2)
    eh = jnp.broadcast_to(jnp.concatenate([ef, eb], axis=1)[:, :, None], (n_t, 4, LANE)).astype(F32)
    lane_is_b = (np.arange(LANE) // (RET_DK // 2)) % 2
    el = jnp.stack([ef[:, lane_is_b], eb[:, lane_is_b]], axis=1).astype(F32)
    tok = lambda length: pl.BlockSpec((1, length, LANE), lambda bi, t: (bi, 0, t))
    return pl.pallas_call(
        _ret_kernel,
        grid=(b, n_t),
        in_specs=[tok(s), tok(s), tok(s), tok(s), tok(lc), tok(lc),
                  pl.BlockSpec((1, 4, LANE), lambda bi, t: (t, 0, 0)),
                  pl.BlockSpec((1, 2, LANE), lambda bi, t: (t, 0, 0)),
                  pl.BlockSpec((1, LANE), lambda bi, t: (0, t))],
        out_specs=tok(s),
        out_shape=jax.ShapeDtypeStruct((b, s, n), BF16),
        scratch_shapes=[pltpu.VMEM((s // RET_CHUNK, RET_CHUNK, LANE), F32)],
        compiler_params=_cparams(("arbitrary", "arbitrary")),
        name="ret",
    )(rq, rk, rv, rg, rk_c, rv_c, eh, el, g_ret.reshape(1, n))


def _outproj_kernel(oa_ref, yr_ref, x_ref, mod_ref, wo_ref, gffn_ref, wr_ref, x1_ref, h2_ref, aff_ref):
    n_a = oa_ref.shape[2]
    tm = x_ref.shape[1]
    th = tm // OUTPROJ_SPLIT
    groups = [slice(g * th, (g + 1) * th) for g in range(OUTPROJ_SPLIT)]
    ys = [_dot(oa_ref[0, rows, :], wo_ref[0:n_a, :]) + _dot(yr_ref[0, rows, :], wo_ref[n_a:, :]) for rows in groups]
    h2s = []
    for g, (rows, y) in enumerate(zip(groups, ys)):
        x1 = x_ref[0, rows, :] + mod_ref[0, 2:3, :] * y
        x1_ref[0, rows, :] = x1
        h2 = _rms(x1) * gffn_ref[...] * (1.0 + mod_ref[0, 4:5, :]) + mod_ref[0, 3:4, :]
        for r in range(ROW_TILE):
            h2_ref[0, pl.ds(g * th * ROW_TILE + r, th, stride=ROW_TILE), :] = h2[:, r * LANE:(r + 1) * LANE]
        h2s.append(h2)
    n_e = aff_ref.shape[1]
    wr = wr_ref[...]
    w_hi = wr.astype(BF16)
    w_lo = (wr - w_hi.astype(F32)).astype(BF16)
    all_logits = []
    for h2 in h2s:
        h_hi = h2.astype(BF16)
        h_lo = (h2 - h_hi.astype(F32)).astype(BF16)
        lg = _dot(h_hi, w_hi) + (_dot(h_lo, w_hi) + _dot(h_hi, w_lo))
        all_logits.append(lg.T[:n_e])
    for rows, logits in zip(groups, all_logits):
        ex = jnp.exp(logits - jnp.max(logits, axis=0, keepdims=True))
        aff_ref[0, :, rows] = ex / jnp.sum(ex, axis=0, keepdims=True)


def _outproj(o_att, y_ret, x, mod, w_o, g_ffn, w_router):
    b, s, d = x.shape
    tm = min(s, 512)
    n_e = w_router.shape[1]
    assert n_e <= LANE
    w_router_t = jnp.pad(w_router, ((0, 0), (0, LANE - n_e)))
    tok = lambda n: pl.BlockSpec((1, tm, n), lambda bi, i: (bi, i, 0))
    full = lambda shape: pl.BlockSpec(shape, lambda bi, i: tuple(0 for _ in shape))
    return pl.pallas_call(
        _outproj_kernel,
        grid=(b, s // tm),
        in_specs=[tok(o_att.shape[2]), tok(y_ret.shape[2]), tok(d),
                  pl.BlockSpec((1, 6, d), lambda bi, i: (bi, 0, 0)),
                  full(w_o.shape), full((1, d)), full(w_router_t.shape)],
        out_specs=[tok(d), pl.BlockSpec((1, tm * ROW_TILE, LANE), lambda bi, i: (bi, i, 0)),
                   pl.BlockSpec((1, n_e, tm), lambda bi, i: (bi, 0, i))],
        out_shape=[jax.ShapeDtypeStruct((b, s, d), F32), jax.ShapeDtypeStruct((b, s * ROW_TILE, LANE), F32),
                   jax.ShapeDtypeStruct((b, n_e, s), F32)],
        compiler_params=_cparams(("arbitrary", "arbitrary")),
        name="outproj",
    )(o_att, y_ret, x, mod, w_o, g_ffn, w_router_t)


def _topk_kernel(aff_ref, idx_ref, c_ref, t_ref, *, cap):
    n_e, n_r, _ = c_ref.shape
    assert n_r <= LANE
    a = aff_ref[0]

    def count(mask):
        ones = jnp.where(mask, 1, 0)
        return jnp.sum(jnp.sum(ones, axis=1, keepdims=True), axis=2, keepdims=True)

    def as_float(bits):
        return pltpu.bitcast(bits, F32)

    def search(i, lo):
        cand = lo | jnp.left_shift(jnp.int32(1), 30 - i)
        return jnp.where(count(a >= as_float(cand)) >= cap, cand, lo)

    thr = lax.fori_loop(0, 31, search, jnp.zeros((n_e, 1, 1), jnp.int32))
    above = a >= as_float(thr + 1)
    need = cap - count(above)
    window = (a >= as_float(thr)) & jnp.logical_not(above)
    tok = (lax.broadcasted_iota(jnp.int32, (1, n_r, LANE), 1) * LANE
           + lax.broadcasted_iota(jnp.int32, (1, n_r, LANE), 2))

    def surplus(state):
        return jnp.max(state[1] - need) > 0

    def drop_one(state):
        win, n_win = state
        active = n_win > need
        inside = win > 0.5
        low = jnp.min(jnp.min(jnp.where(inside, a, jnp.inf), axis=1, keepdims=True), axis=2, keepdims=True)
        cand = inside & (a == low)
        last = jnp.max(jnp.max(jnp.where(cand, tok, -1), axis=1, keepdims=True), axis=2, keepdims=True)
        win = jnp.where(cand & (tok == last) & active, 0.0, win)
        return win, n_win - jnp.where(active, 1, 0)

    win, _ = lax.while_loop(surplus, drop_one, (jnp.where(window, 1.0, 0.0), count(window)))
    sel = above | (win > 0.5)

    li = lax.broadcasted_iota(jnp.int32, (LANE, LANE), 0)
    lj = lax.broadcasted_iota(jnp.int32, (LANE, LANE), 1)
    tri = jnp.where(li <= lj, 1.0, 0.0).astype(BF16)
    ones = jnp.ones((LANE, LANE), BF16)
    n_flat = n_e * n_r
    m2 = jnp.where(sel, 1.0, 0.0).astype(BF16).reshape(n_flat, LANE)
    within = _dot(m2, tri)
    tot = _dot(m2, ones)
    pi = lax.broadcasted_iota(jnp.int32, (n_flat, n_flat), 0)
    pj = lax.broadcasted_iota(jnp.int32, (n_flat, n_flat), 1)
    before = jnp.where((pi // n_r == pj // n_r) & (pj < pi), 1.0, 0.0).astype(BF16)
    off = _dot(before, tot.astype(BF16))
    c_ref[...] = (within + off).reshape(n_e, n_r, LANE)
    t_ref[...] = tot.reshape(n_e, n_r, LANE)

    incl = jnp.where(li <= lj, 1.0, 0.0).astype(BF16)[:n_r]
    jcol = lax.broadcasted_iota(jnp.int32, (cap, LANE), 0).astype(F32)
    lane = lax.broadcasted_iota(jnp.int32, (cap, LANE), 1)
    ones8 = jnp.ones((8, LANE), BF16)
    pad = jnp.zeros((LANE - n_r, LANE), F32)

    eye = jnp.where(li == lj, 1.0, 0.0).astype(BF16)[:n_r]

    def expert_group(i, _):
        es = [i * TOPK_GROUP + u for u in range(TOPK_GROUP)]
        tots = [t_ref[e].astype(BF16) for e in es]
        ends = [_dot_tn(t, incl)[0:1, :] for t in tots]
        starts = [end - _dot_tn(t, eye)[0:1, :] for t, end in zip(tots, ends)]
        in_rows = [(start <= jcol) & (end > jcol) & (lane < n_r) for start, end in zip(starts, ends)]
        c_pads = [jnp.concatenate([c_ref[e], pad], axis=0) if n_r < LANE else c_ref[e] for e in es]
        c_rows = [jnp.dot(jnp.where(m, 1.0, 0.0), c, preferred_element_type=F32, precision=HIGHEST)
                  for m, c in zip(in_rows, c_pads)]
        n_rows = [_dot_nt(ones8, jnp.where((end <= jcol) & (lane < n_r), 1.0, 0.0).astype(BF16)) for end in ends]
        n_lanes = [_dot_nt(ones8, jnp.where(c <= jcol, 1.0, 0.0).astype(BF16)) for c in c_rows]
        for e, nr, nl in zip(es, n_rows, n_lanes):
            idx_ref[0, pl.ds(e, 1), :] = (nr[0:1, :] * LANE + nl[0:1, :]).astype(jnp.int32)
        return 0

    assert n_e % TOPK_GROUP == 0
    lax.fori_loop(0, n_e // TOPK_GROUP, expert_group, 0)


def _topk(aff, cap):
    b, n_e, s = aff.shape
    n_r = s // LANE
    aff4 = aff.reshape(b, n_e, n_r, LANE)
    return pl.pallas_call(
        functools.partial(_topk_kernel, cap=cap),
        grid=(b,),
        in_specs=[pl.BlockSpec((1, n_e, n_r, LANE), lambda bi: (bi, 0, 0, 0))],
        out_specs=pl.BlockSpec((1, n_e, cap), lambda bi: (bi, 0, 0)),
        out_shape=jax.ShapeDtypeStruct((b, n_e, cap), jnp.int32),
        scratch_shapes=[pltpu.VMEM((n_e, n_r, LANE), F32), pltpu.VMEM((n_e, n_r, LANE), F32)],
        compiler_params=_cparams(("arbitrary",)),
        name="topk",
    )(aff4)


def _ffn_kernel(idx_ref, aff_ref, h2_hbm, wg_ref, wu_ref, wd_ref, out_hbm,
                h2_v, acc_v, sem, *part_bufs, tm):
    bi = pl.program_id(0)
    e = pl.program_id(1)
    cap = idx_ref.shape[2]
    n_parts = cap // tm
    xs_bufs, y_bufs = part_bufs[:n_parts], part_bufs[n_parts:]

    @pl.when(e == 0)
    def _():
        cp = pltpu.make_async_copy(h2_hbm.at[bi], h2_v, sem.at[0])
        cp.start()
        acc_v[...] = jnp.zeros_like(acc_v)
        cp.wait()

    def tile(t):
        return pl.ds(pl.multiple_of(t * ROW_TILE, ROW_TILE), ROW_TILE)

    def gather(part):
        for j in range(tm):
            xs_bufs[part][j * ROW_TILE:(j + 1) * ROW_TILE, :] = h2_v[tile(idx_ref[0, 0, part * tm + j]), :]

    def experts(part):
        xb = jnp.concatenate([xs_bufs[part][pl.ds(r, tm, stride=ROW_TILE), :] for r in range(ROW_TILE)],
                             axis=1).astype(BF16)
        a = _dot(xb, wg_ref[0])
        u = _dot(xb, wu_ref[0])
        y = _dot((_silu(a) * u).astype(BF16), wd_ref[0])
        for r in range(ROW_TILE):
            y_bufs[part][pl.ds(r, tm, stride=ROW_TILE), :] = y[:, r * LANE:(r + 1) * LANE]

    def scatter(part):
        for j0 in range(0, tm, FFN_UNROLL):
            toks = [idx_ref[0, 0, part * tm + j0 + u] for u in range(FFN_UNROLL)]
            rows = [acc_v[tile(t), :]
                    + y_bufs[part][(j0 + u) * ROW_TILE:(j0 + u + 1) * ROW_TILE, :] * aff_ref[0, 0, t]
                    for u, t in enumerate(toks)]
            for t, row in zip(toks, rows):
                acc_v[tile(t), :] = row

    gather(0)
    for part in range(n_parts):
        experts(part)
        if part + 1 < n_parts:
            gather(part + 1)
        scatter(part)

    @pl.when(e == pl.num_programs(1) - 1)
    def _():
        cp = pltpu.make_async_copy(acc_v, out_hbm.at[bi], sem.at[1])
        cp.start()
        cp.wait()


def _ffn(idx, aff, h2, w_gate, w_up, w_down):
    b, rows, _ = h2.shape
    s = rows // ROW_TILE
    n_e, d, f = w_gate.shape
    cap = idx.shape[2]
    tm = min(cap, 256)
    assert d == ROW_TILE * LANE and tm % FFN_UNROLL == 0
    smem = lambda n: pl.BlockSpec((1, 1, n), lambda bi, e: (bi * n_e + e, 0, 0), memory_space=pltpu.SMEM)
    idx = idx.reshape(b * n_e, 1, cap)
    aff = aff.reshape(b * n_e, 1, s)
    return pl.pallas_call(
        functools.partial(_ffn_kernel, tm=tm),
        grid=(b, n_e),
        in_specs=[smem(cap), smem(s),
                  pl.BlockSpec(memory_space=pl.ANY),
                  pl.BlockSpec((1, d, f), lambda bi, e: (e, 0, 0)),
                  pl.BlockSpec((1, d, f), lambda bi, e: (e, 0, 0)),
                  pl.BlockSpec((1, f, d), lambda bi, e: (e, 0, 0))],
        out_specs=pl.BlockSpec(memory_space=pl.ANY),
        out_shape=jax.ShapeDtypeStruct((b, rows, LANE), F32),
        scratch_shapes=[pltpu.VMEM((rows, LANE), F32), pltpu.VMEM((rows, LANE), F32),
                        pltpu.SemaphoreType.DMA((2,))]
        + [pltpu.VMEM((tm * ROW_TILE, LANE), F32)] * (2 * (cap // tm)),
        compiler_params=_cparams(("arbitrary", "arbitrary"), FFN_VMEM_LIMIT),
        name="ffn",
    )(idx, aff, h2, w_gate, w_up, w_down)


def _final_kernel(x1_ref, acc_ref, mod_ref, g_ref, o_ref):
    tm = x1_ref.shape[1]
    acc = jnp.concatenate([acc_ref[0, pl.ds(r, tm, stride=ROW_TILE), :] for r in range(ROW_TILE)], axis=1)
    x2 = x1_ref[0] + mod_ref[0, 5:6, :] * acc
    o_ref[0] = _rms(x2) * g_ref[...]


def _final(x1, acc, mod, g_final):
    b, s, d = x1.shape
    tm = min(s, 512)
    tok = pl.BlockSpec((1, tm, d), lambda bi, i: (bi, i, 0))
    return pl.pallas_call(
        _final_kernel,
        grid=(b, s // tm),
        in_specs=[tok, pl.BlockSpec((1, tm * ROW_TILE, LANE), lambda bi, i: (bi, i, 0)),
                  pl.BlockSpec((1, 6, d), lambda bi, i: (bi, 0, 0)),
                  pl.BlockSpec((1, d), lambda bi, i: (0, 0))],
        out_specs=tok,
        out_shape=jax.ShapeDtypeStruct((b, s, d), F32),
        compiler_params=_cparams(("arbitrary", "arbitrary")),
        name="final",
    )(x1, acc, mod, g_final.reshape(1, d))


def kernel(x, c, ctx, c_ctx, w_ada, b_ada, g_norm_mix, g_norm_ffn, w_in, g_q_lora, g_kv_lora, w_uq, w_ukv,
           ret_exp_fwd, ret_exp_bwd, g_ret, w_o, w_router, w_exp_gate, w_exp_up, w_exp_down, g_final):
    b, s, d = x.shape
    depth = w_ada.shape[0]
    assert depth == 1, "the context stream update is only needed for depth > 1"
    assert s % LANE == 0 and s % GRID_W == 0
    cap = EC_CAPACITY * s // N_EXPERTS

    rows = -(-(b + 1) // 8) * 8
    cc = jnp.zeros((rows, d), F32).at[:b].set(c).at[b].set(c_ctx)
    mods = _ada(cc, w_ada[0], b_ada[0])
    mod = mods[:b].reshape(b, 6, d)
    mod_c = mods[b:b + 1].reshape(1, 6, d)

    w_in_ext, w_uq_ext, w_ukv_ext = _prep_weights(w_in[0], w_uq[0], w_ukv[0])
    tabs = _rope_tables(s)
    gmix = g_norm_mix[0].reshape(1, d)
    gq = g_q_lora[0].reshape(1, -1)
    gkv = g_kv_lora[0].reshape(1, -1)

    qa, ka, va, rq, rk, rv, rg = _inproj(x, mod, gmix, w_in_ext, gq, gkv, w_uq_ext, w_ukv_ext, tabs, is_ctx=False)
    ka_c, va_c, rk_c, rv_c = _inproj(ctx, mod_c, gmix, w_in_ext, gq, gkv, w_uq_ext, w_ukv_ext, None, is_ctx=True)

    o_att = _attention(qa, ka, va, ka_c, va_c)
    y_ret = _retention(rq, rk, rv, rg, rk_c, rv_c, ret_exp_fwd[0], ret_exp_bwd[0], g_ret[0])

    x1, h2, aff = _outproj(o_att, y_ret, x, mod, w_o[0].astype(BF16), g_norm_ffn[0].reshape(1, d),
                           w_router[0])
    idx = _topk(aff, cap)
    acc = _ffn(idx, aff, h2, w_exp_gate[0].astype(BF16), w_exp_up[0].astype(BF16), w_exp_down[0].astype(BF16))
    return _final(x1, acc, mod, g_final)
```

```python
import functools

import numpy as np
import jax
import jax.numpy as jnp
from jax import lax
from jax.experimental import pallas as pl
from jax.experimental.pallas import tpu as pltpu

GRID_W = 64
MLA_HEADS = 8
MLA_Q_RANK = 256
MLA_KV_RANK = 128
MLA_NOPE = 64
MLA_ROPE = 32
MLA_V = 64
RET_HEADS = 8
RET_DK = 64
RET_DV = 64
RET_CHUNK = 128
N_EXPERTS = 16
EC_CAPACITY = 2
ROPE_BASE = 10000.0
EPS = 1e-6

LANE = 128
HEAD_TILE = 128
ATTN_TQ = 256
ATTN_SUB = 8
ATTN_KEY_BLOCK = 128
OUTPROJ_SPLIT = 2
TOPK_GROUP = 4
RET_UNROLL = 16
ROW_TILE = 8
FFN_UNROLL = 8
VMEM_LIMIT = 48 * 1024 * 1024
FFN_VMEM_LIMIT = 60 * 1024 * 1024
LOG2E = 1.4426950408889634

F32 = jnp.float32
BF16 = jnp.bfloat16
HIGHEST = lax.Precision.HIGHEST

W_CQ = (0, 256)
W_CKV = (256, 384)
W_KR = (384, 512)
W_RQ = (512, 1024)
W_RK = (1024, 1536)
W_RV = (1536, 2048)
W_RG = (2048, 2560)
D_IN_EXT = 2560


def _cparams(sem, limit=VMEM_LIMIT):
    return pltpu.CompilerParams(dimension_semantics=sem, vmem_limit_bytes=limit)


def _dot(a, b):
    return jnp.dot(a, b, preferred_element_type=F32)


def _dot_nt(a, b):
    return lax.dot_general(a, b, (((1,), (1,)), ((), ())), preferred_element_type=F32)


def _dot_tn(a, b):
    return lax.dot_general(a, b, (((0,), (0,)), ((), ())), preferred_element_type=F32)


def _silu(x):
    return x * jax.nn.sigmoid(x)


def _mla_lane_maps():
    half = MLA_ROPE // 2
    src = np.zeros(HEAD_TILE, np.int32)
    valid = np.zeros(HEAD_TILE, bool)
    src[0:half] = MLA_NOPE + np.arange(half)
    valid[0:half] = True
    src[half:64] = np.arange(64 - half)
    valid[half:64] = True
    src[64:64 + half] = MLA_NOPE + half + np.arange(half)
    valid[64:64 + half] = True
    n_rest = MLA_NOPE - (64 - half)
    src[64 + half:64 + half + n_rest] = (64 - half) + np.arange(n_rest)
    valid[64 + half:64 + half + n_rest] = True
    return src, valid


def _prep_weights(w_in, w_uq, w_ukv):
    d_model = w_in.shape[0]
    half = MLA_ROPE // 2
    src, valid = _mla_lane_maps()
    is_rope = valid & (src >= MLA_NOPE)
    is_nope = valid & (src < MLA_NOPE)

    o = 0
    cq = w_in[:, o:o + MLA_Q_RANK]; o += MLA_Q_RANK
    ckv = w_in[:, o:o + MLA_KV_RANK]; o += MLA_KV_RANK
    kr = w_in[:, o:o + MLA_ROPE]; o += MLA_ROPE
    n_r = RET_HEADS * RET_DK
    rq = w_in[:, o:o + n_r]; o += n_r
    rk = w_in[:, o:o + n_r]; o += n_r
    rv = w_in[:, o:o + RET_HEADS * RET_DV]; o += RET_HEADS * RET_DV
    rg = w_in[:, o:o + RET_HEADS * RET_DV]

    kr_idx = np.where(is_rope, src - MLA_NOPE, 0)
    kr_ext = jnp.where(jnp.asarray(is_rope)[None, :], kr[:, kr_idx], 0.0)

    hd = RET_DK // 2
    lanes = np.arange(n_r)
    tile, l = lanes // LANE, lanes % LANE
    grp = l // hd
    head = 2 * tile + (grp % 2)
    perm = head * RET_DK + (grp // 2) * hd + (l % hd)
    rq_p = rq[:, perm]
    rk_p = rk[:, perm] * (RET_DK ** -0.5)

    w_in_ext = jnp.concatenate([cq, ckv, kr_ext, rq_p, rk_p, rv, rg], axis=1).astype(BF16)
    assert w_in_ext.shape == (d_model, D_IN_EXT)

    dq = MLA_NOPE + MLA_ROPE
    q_cols = (np.arange(MLA_HEADS)[:, None] * dq + src[None, :]).reshape(-1)
    q_valid = np.tile(valid, MLA_HEADS)
    w_uq_ext = jnp.where(jnp.asarray(q_valid)[None, :], w_uq[:, q_cols], 0.0).astype(BF16)

    dkv = MLA_NOPE + MLA_V
    k_cols = (np.arange(MLA_HEADS)[:, None] * dkv + np.where(is_nope, src, 0)[None, :]).reshape(-1)
    k_valid = np.tile(is_nope, MLA_HEADS)
    w_uk_ext = jnp.where(jnp.asarray(k_valid)[None, :], w_ukv[:, k_cols], 0.0)
    v_cols = (np.arange(MLA_HEADS)[:, None] * dkv + MLA_NOPE + np.arange(MLA_V)[None, :]).reshape(-1)
    w_uv = w_ukv[:, v_cols]
    w_ukv_ext = jnp.concatenate([w_uk_ext, w_uv], axis=1).astype(BF16)
    return w_in_ext, w_uq_ext, w_ukv_ext


def _rope_patterns():
    pat = np.zeros((8, LANE), np.float32)
    half = MLA_ROPE // 2
    nf = MLA_ROPE // 4
    inv = ROPE_BASE ** (-np.arange(nf, dtype=np.float64) / nf)
    for base, sign in ((0, -1.0), (64, 1.0)):
        for i in range(half):
            pat[0, base + i] = inv[i % nf]
            pat[1, base + i] = 1.0 if i < nf else 0.0
            pat[2, base + i] = sign
    hd = RET_DK // 2
    nf = RET_DK // 4
    inv = ROPE_BASE ** (-np.arange(nf, dtype=np.float64) / nf)
    for l in range(LANE):
        i = l % hd
        pat[3, l] = inv[i % nf]
        pat[4, l] = 1.0 if i < nf else 0.0
        pat[5, l] = -1.0 if l < 64 else 1.0
    return jnp.asarray(pat)


def _ada_kernel(c_ref, w_ref, b_ref, o_ref):
    s = _silu(c_ref[...])
    o_ref[...] = jnp.dot(s, w_ref[...], preferred_element_type=F32, precision=HIGHEST) + b_ref[...]


def _ada(cc, w_ada, b_ada):
    rows, d = cc.shape
    n = w_ada.shape[1]
    tn = 1024
    return pl.pallas_call(
        _ada_kernel,
        grid=(n // tn,),
        in_specs=[pl.BlockSpec((rows, d), lambda j: (0, 0)),
                  pl.BlockSpec((d, tn), lambda j: (0, j)),
                  pl.BlockSpec((1, tn), lambda j: (0, j))],
        out_specs=pl.BlockSpec((rows, tn), lambda j: (0, j)),
        out_shape=jax.ShapeDtypeStruct((rows, n), F32),
        compiler_params=_cparams(("arbitrary",)),
        name="ada",
    )(cc, w_ada, b_ada.reshape(1, n))


def _rope_kernel(pat_ref, o_ref, *, tm):
    t = pl.program_id(0) * tm + lax.broadcasted_iota(jnp.int32, (tm, LANE), 0)
    row = (t // GRID_W).astype(F32)
    col = (t % GRID_W).astype(F32)
    for k in range(2):
        inv = pat_ref[3 * k:3 * k + 1, :]
        use_row = pat_ref[3 * k + 1:3 * k + 2, :]
        sign = pat_ref[3 * k + 2:3 * k + 3, :]
        ang = jnp.where(use_row > 0.5, row, col) * inv
        active = sign != 0.0
        o_ref[2 * k] = jnp.where(active, jnp.cos(ang), 1.0)
        o_ref[2 * k + 1] = jnp.where(active, sign * jnp.sin(ang), 0.0)


def _rope_tables(seq):
    tm = min(seq, 512)
    return pl.pallas_call(
        functools.partial(_rope_kernel, tm=tm),
        grid=(seq // tm,),
        in_specs=[pl.BlockSpec((8, LANE), lambda i: (0, 0))],
        out_specs=pl.BlockSpec((4, tm, LANE), lambda i: (0, i, 0)),
        out_shape=jax.ShapeDtypeStruct((4, seq, LANE), F32),
        compiler_params=_cparams(("arbitrary",)),
        name="rope",
    )(_rope_patterns())


def _rms(x):
    return x * lax.rsqrt(jnp.mean(x * x, axis=-1, keepdims=True) + EPS)


def _rot(x, cos, sin):
    return x * cos + pltpu.roll(x, 64, axis=1) * sin


def _inproj_kernel(*refs, is_ctx):
    if is_ctx:
        (x_ref, mod_ref, gmix_ref, win_ref, gkv_ref, wukv_ref,
         ka_ref, va_ref, rk_ref, rv_ref) = refs
    else:
        (x_ref, mod_ref, gmix_ref, win_ref, gq_ref, gkv_ref, wuq_ref, wukv_ref, tab_ref,
         qa_ref, ka_ref, va_ref, rq_ref, rk_ref, rv_ref, rg_ref) = refs
    n_k = MLA_HEADS * HEAD_TILE
    h = _rms(x_ref[0]) * gmix_ref[...] * (1.0 + mod_ref[0, 1:2, :]) + mod_ref[0, 0:1, :]
    z = _dot(h.astype(BF16), win_ref[...])
    ckv = _rms(z[:, W_CKV[0]:W_CKV[1]]) * gkv_ref[...]
    kv = _dot(ckv.astype(BF16), wukv_ref[...])
    kr = z[:, W_KR[0]:W_KR[1]]
    if not is_ctx:
        cm, sm, cr, sr = tab_ref[0], tab_ref[1], tab_ref[2], tab_ref[3]
        kr = _rot(kr, cm, sm)
    for hh in range(MLA_HEADS):
        ka_ref[0, :, hh * HEAD_TILE:(hh + 1) * HEAD_TILE] = (
            kv[:, hh * HEAD_TILE:(hh + 1) * HEAD_TILE] + kr).astype(BF16)
    for hp in range(MLA_HEADS // 2):
        va_ref[0, hp, 0] = kv[:, n_k + hp * LANE:n_k + (hp + 1) * LANE].T.astype(BF16)
    rv_ref[0] = z[:, W_RV[0]:W_RV[1]].astype(BF16)
    n_t = (W_RK[1] - W_RK[0]) // LANE
    if is_ctx:
        rk_ref[0] = z[:, W_RK[0]:W_RK[1]].astype(BF16)
        return
    for t in range(n_t):
        rk_ref[0, :, t * LANE:(t + 1) * LANE] = _rot(
            z[:, W_RK[0] + t * LANE:W_RK[0] + (t + 1) * LANE], cr, sr).astype(BF16)
        rq_ref[0, :, t * LANE:(t + 1) * LANE] = _rot(
            z[:, W_RQ[0] + t * LANE:W_RQ[0] + (t + 1) * LANE], cr, sr).astype(BF16)
    rg_ref[0] = z[:, W_RG[0]:W_RG[1]].astype(BF16)
    cq = _rms(z[:, W_CQ[0]:W_CQ[1]]) * gq_ref[...]
    q = _dot(cq.astype(BF16), wuq_ref[...])
    qscale = (MLA_NOPE + MLA_ROPE) ** -0.5 * LOG2E
    for hh in range(MLA_HEADS):
        qh = _rot(q[:, hh * HEAD_TILE:(hh + 1) * HEAD_TILE], cm, sm)
        qa_ref[0, :, hh * HEAD_TILE:(hh + 1) * HEAD_TILE] = (qh * qscale).astype(BF16)


def _inproj(x, mod, gmix, w_in_ext, gq, gkv, w_uq_ext, w_ukv_ext, tabs, *, is_ctx):
    b, s, d = x.shape
    tm = min(s, 512)
    n_k = MLA_HEADS * HEAD_TILE
    n_r = RET_HEADS * RET_DK
    hp = MLA_HEADS // 2
    per_b = mod.shape[0] > 1
    full = lambda shape: pl.BlockSpec(shape, lambda bi, i: tuple(0 for _ in shape))
    tok = lambda n: pl.BlockSpec((1, tm, n), lambda bi, i: (bi, i, 0))
    tok_shape = lambda n: jax.ShapeDtypeStruct((b, s, n), BF16)
    mod_spec = pl.BlockSpec((1, 6, d), (lambda bi, i: (bi, 0, 0)) if per_b else (lambda bi, i: (0, 0, 0)))
    vt_spec = pl.BlockSpec((1, hp, 1, LANE, tm), lambda bi, i: (bi, 0, i, 0, 0))
    vt_shape = jax.ShapeDtypeStruct((b, hp, s // tm, LANE, tm), BF16)
    if is_ctx:
        ins = [x, mod, gmix, w_in_ext, gkv, w_ukv_ext]
        in_specs = [tok(d), mod_spec, full((1, d)), full(w_in_ext.shape), full((1, MLA_KV_RANK)),
                    full(w_ukv_ext.shape)]
        outs = [(tok(n_k), tok_shape(n_k)), (vt_spec, vt_shape)] + [(tok(n_r), tok_shape(n_r))] * 2
    else:
        ins = [x, mod, gmix, w_in_ext, gq, gkv, w_uq_ext, w_ukv_ext, tabs]
        in_specs = [tok(d), mod_spec, full((1, d)), full(w_in_ext.shape), full((1, MLA_Q_RANK)),
                    full((1, MLA_KV_RANK)), full(w_uq_ext.shape), full(w_ukv_ext.shape),
                    pl.BlockSpec((4, tm, LANE), lambda bi, i: (0, i, 0))]
        outs = [(tok(n_k), tok_shape(n_k))] * 2 + [(vt_spec, vt_shape)] + [(tok(n_r), tok_shape(n_r))] * 4
    return pl.pallas_call(
        functools.partial(_inproj_kernel, is_ctx=is_ctx),
        grid=(b, s // tm),
        in_specs=in_specs,
        out_specs=[o[0] for o in outs],
        out_shape=[o[1] for o in outs],
        compiler_params=_cparams(("arbitrary", "arbitrary")),
        name="inproj_ctx" if is_ctx else "inproj",
    )(*ins)


def _attn_kernel(q_ref, kl_ref, kc_ref, vl_ref, vc_ref, o_ref, sa_ref, sb_ref, *, tk, tq):
    n_sub = q_ref.shape[1] // tq
    n_chunks = kl_ref.shape[1] // tk
    lc = kc_ref.shape[1]
    assert n_chunks >= 2 and n_chunks % 2 == 0 and lc <= tk
    streams = tuple((sub * tq, hh * HEAD_TILE) for sub in range(n_sub) for hh in range(2))

    def scores_into(s_ref, k2):
        maxes = []
        for st, (q0, lo) in enumerate(streams):
            s = _dot_nt(k2[:, lo:lo + HEAD_TILE], q_ref[0, q0:q0 + tq, lo:lo + HEAD_TILE])
            s_ref[st, 0:k2.shape[0], :] = s
            maxes.append(jnp.max(s, axis=0, keepdims=True))
        return tuple(maxes)

    def consume(carries, s_ref, n_keys, maxes, vt):
        out = []
        for hh in range(len(streams)):
            m, l, acc = carries[hh]
            m_new = jnp.maximum(m, maxes[hh])
            alpha = jnp.exp2(m - m_new)
            blocks, l_add = [], jnp.zeros_like(l)
            for r in range(0, n_keys, ATTN_KEY_BLOCK):
                p = jnp.exp2(s_ref[hh, r:r + ATTN_KEY_BLOCK, :] - m_new)
                l_add = l_add + jnp.sum(p, axis=0, keepdims=True)
                blocks.append(p.astype(BF16))
            acc = alpha * acc + _dot(vt, jnp.concatenate(blocks, axis=0))
            out.append((m_new, alpha * l + l_add, acc))
        return tuple(out)

    def keys(c):
        return kl_ref[0, pl.ds(pl.multiple_of(c * tk, tk), tk), :]

    def body(i, state):
        carries, max_a = state
        c = 2 * i
        max_b = scores_into(sb_ref, keys(c + 1))
        carries = consume(carries, sa_ref, tk, max_a, vl_ref[0, 0, c])
        max_a = scores_into(sa_ref, keys(c + 2))
        carries = consume(carries, sb_ref, tk, max_b, vl_ref[0, 0, c + 1])
        return carries, max_a

    init = (jnp.full((1, tq), -jnp.inf, F32), jnp.zeros((1, tq), F32), jnp.zeros((LANE, tq), F32))
    state = ((init,) * len(streams), scores_into(sa_ref, kl_ref[0, 0:tk, :]))
    carries, max_a = lax.fori_loop(0, (n_chunks - 2) // 2, body, state)
    max_b = scores_into(sb_ref, kl_ref[0, (n_chunks - 1) * tk:n_chunks * tk, :])
    carries = consume(carries, sa_ref, tk, max_a, vl_ref[0, 0, n_chunks - 2])
    max_c = scores_into(sa_ref, kc_ref[0])
    carries = consume(carries, sb_ref, tk, max_b, vl_ref[0, 0, n_chunks - 1])
    carries = consume(carries, sa_ref, lc, max_c, vc_ref[0, 0, 0])
    for sub in range(n_sub):
        (_, l_a, acc_a), (_, l_b, acc_b) = carries[2 * sub], carries[2 * sub + 1]
        out_t = jnp.concatenate([(acc_a / l_a)[:MLA_V], (acc_b / l_b)[MLA_V:]], axis=0)
        o_ref[0, sub * tq:(sub + 1) * tq, :] = out_t.T.astype(BF16)


def _attention(qa, ka, vt, ka_c, vt_c):
    b, s, _ = qa.shape
    lc = ka_c.shape[1]
    hp, n_chunks, _, tk = vt.shape[1:]
    tq = min(s, ATTN_TQ)
    tqb = min(s, ATTN_SUB * tq)
    n_streams = 2 * (tqb // tq)
    return pl.pallas_call(
        functools.partial(_attn_kernel, tk=tk, tq=tq),
        grid=(b, hp, s // tqb),
        in_specs=[pl.BlockSpec((1, tqb, 2 * HEAD_TILE), lambda bi, h, i: (bi, i, h)),
                  pl.BlockSpec((1, s, 2 * HEAD_TILE), lambda bi, h, i: (bi, 0, h)),
                  pl.BlockSpec((1, lc, 2 * HEAD_TILE), lambda bi, h, i: (bi, 0, h)),
                  pl.BlockSpec((1, 1, n_chunks, LANE, tk), lambda bi, h, i: (bi, h, 0, 0, 0)),
                  pl.BlockSpec((1, 1, 1, LANE, lc), lambda bi, h, i: (bi, h, 0, 0, 0))],
        out_specs=pl.BlockSpec((1, tqb, 2 * MLA_V), lambda bi, h, i: (bi, i, h)),
        out_shape=jax.ShapeDtypeStruct((b, s, MLA_HEADS * MLA_V), BF16),
        scratch_shapes=[pltpu.VMEM((n_streams, tk, tq), F32), pltpu.VMEM((n_streams, tk, tq), F32)],
        compiler_params=_cparams(("arbitrary", "arbitrary", "arbitrary")),
        name="attn",
    )(qa, ka, ka_c, vt, vt_c)


def _ret_kernel(q_ref, k_ref, v_ref, g_ref, kc_ref, vc_ref, eh_ref, el_ref, gret_ref, o_ref, rb_ref):
    c = RET_CHUNK
    n_lat = q_ref.shape[1] // c
    n_ctx = kc_ref.shape[1] // c

    def log_gamma(e):
        return jnp.log1p(-jnp.exp2(-e))

    lgf_a, lgf_b = log_gamma(eh_ref[0, 0:1, :]), log_gamma(eh_ref[0, 1:2, :])
    lgb_a, lgb_b = log_gamma(eh_ref[0, 2:3, :]), log_gamma(eh_ref[0, 3:4, :])
    lgf_l, lgb_l = log_gamma(el_ref[0, 0:1, :]), log_gamma(el_ref[0, 1:2, :])

    ri = lax.broadcasted_iota(jnp.int32, (c, c), 0)
    ci = lax.broadcasted_iota(jnp.int32, (c, c), 1)
    diff = (ri - ci).astype(F32)

    def decay(lgf, lgb):
        return jnp.where(diff >= 0, jnp.exp(lgf * diff), jnp.exp(-lgb * diff))

    d_a, d_b = decay(lgf_a, lgb_a), decay(lgf_b, lgb_b)
    pos = lax.broadcasted_iota(jnp.int32, (c, LANE), 0).astype(F32)
    xi_f = jnp.exp(lgf_l * (pos + 1.0))
    xi_b = jnp.exp(lgb_l * (c - pos))
    zeta_f = jnp.exp(lgf_l * (c - 1.0 - pos))
    zeta_b = jnp.exp(lgb_l * pos)
    hd = RET_DK // 2
    row_is_a = (ri // hd) % 2 == 0
    col_is_a = ci < RET_DV
    gc_f = jnp.exp(jnp.where(row_is_a, lgf_a, lgf_b) * c)
    gc_b = jnp.exp(jnp.where(row_is_a, lgb_a, lgb_b) * c)
    same_head = (row_is_a == col_is_a).astype(F32)
    lane = lax.broadcasted_iota(jnp.int32, (c, LANE), 1)
    qk_is_a = (lane // hd) % 2 == 0
    v_is_a = lane < RET_DV

    def state_update(r, gc, k, v, zeta):
        return gc * r + _dot_tn((k.astype(F32) * zeta).astype(BF16), v)

    def lat(ref, n):
        return ref[0, pl.ds(pl.multiple_of(n * c, c), c), :]

    zero = jnp.zeros((c, LANE), F32)

    r = zero
    for cc in reversed(range(n_ctx)):
        r = state_update(r, gc_b, kc_ref[0, cc * c:(cc + 1) * c, :], vc_ref[0, cc * c:(cc + 1) * c, :], zeta_b)

    def key_value_sum(k, v, zeta):
        return _dot_tn((k.astype(F32) * zeta).astype(BF16), v)

    def bwd_body(i, r):
        ns = [n_lat - 1 - (i * RET_UNROLL + u) for u in range(RET_UNROLL)]
        sums = [key_value_sum(lat(k_ref, n), lat(v_ref, n), zeta_b) for n in ns]
        for n, kv in zip(ns, sums):
            rb_ref[n] = r
            r = gc_b * r + kv
        return r

    assert n_lat % RET_UNROLL == 0
    lax.fori_loop(0, n_lat // RET_UNROLL, bwd_body, r)

    r = zero
    for cc in range(n_ctx):
        r = state_update(r, gc_f, kc_ref[0, cc * c:(cc + 1) * c, :], vc_ref[0, cc * c:(cc + 1) * c, :], zeta_f)

    gret = gret_ref[...]

    def fwd_body(i, r):
        ns = [i * RET_UNROLL + u for u in range(RET_UNROLL)]
        sums = [key_value_sum(lat(k_ref, n), lat(v_ref, n), zeta_f) for n in ns]
        scores = [chunk_scores(n) for n in ns]
        states = []
        for kv in sums:
            states.append(r)
            r = gc_f * r + kv
        ys = [chunk_mix(n, p2, rn) for n, p2, rn in zip(ns, scores, states)]
        for n, y in zip(ns, ys):
            chunk_store(n, y)
        return r

    def chunk_scores(n):
        q, k = lat(q_ref, n), lat(k_ref, n)
        qz = jnp.zeros_like(q)
        s_a = _dot_nt(jnp.where(qk_is_a, q, qz), k) * d_a
        s_b = _dot_nt(jnp.where(qk_is_a, qz, q), k) * d_b
        return jnp.concatenate([s_a, s_b], axis=1).astype(BF16)

    def chunk_mix(n, p2, r):
        q, v = lat(q_ref, n), lat(v_ref, n)
        vz = jnp.zeros_like(v)
        v2 = jnp.concatenate([jnp.where(v_is_a, v, vz), jnp.where(v_is_a, vz, v)], axis=0)
        qf = q.astype(F32)
        q2 = jnp.concatenate([(qf * xi_f).astype(BF16), (qf * xi_b).astype(BF16)], axis=1)
        r2 = jnp.concatenate([(r * same_head).astype(BF16), (rb_ref[n] * same_head).astype(BF16)], axis=0)
        return _dot(p2, v2) + _dot(q2, r2)

    def chunk_store(n, y):
        inv_n = 1.0 / RET_DV
        sum_a = jnp.sum(jnp.where(v_is_a, y, 0.0), axis=-1, keepdims=True)
        sum_b = jnp.sum(jnp.where(v_is_a, 0.0, y), axis=-1, keepdims=True)
        dlt = y - jnp.where(v_is_a, sum_a, sum_b) * inv_n
        sq = dlt * dlt
        var_a = jnp.sum(jnp.where(v_is_a, sq, 0.0), axis=-1, keepdims=True)
        var_b = jnp.sum(jnp.where(v_is_a, 0.0, sq), axis=-1, keepdims=True)
        yn = dlt * lax.rsqrt(jnp.where(v_is_a, var_a, var_b) * inv_n + EPS) * gret
        o_ref[0, pl.ds(pl.multiple_of(n * c, c), c), :] = (yn * _silu(lat(g_ref, n).astype(F32))).astype(BF16)

    lax.fori_loop(0, n_lat // RET_UNROLL, fwd_body, r)


def _retention(rq, rk, rv, rg, rk_c, rv_c, exp_f, exp_b, g_ret):
    b, s, n = rq.shape
    lc = rk_c.shape[1]
    n_t = n // LANE
    ef = exp_f.reshape(n_t, 2)
    eb = exp_b.reshape(n_t, 2)
    eh = jnp.broadcast_to(jnp.concatenate([ef, eb], axis=1)[:, :, None], (n_t, 4, LANE)).astype(F32)
    lane_is_b = (np.arange(LANE) // (RET_DK // 2)) % 2
    el = jnp.stack([ef[:, lane_is_b], eb[:, lane_is_b]], axis=1).astype(F32)
    tok = lambda length: pl.BlockSpec((1, length, LANE), lambda bi, t: (bi, 0, t))
    return pl.pallas_call(
        _ret_kernel,
        grid=(b, n_t),
        in_specs=[tok(s), tok(s), tok(s), tok(s), tok(lc), tok(lc),
                  pl.BlockSpec((1, 4, LANE), lambda bi, t: (t, 0, 0)),
                  pl.BlockSpec((1, 2, LANE), lambda bi, t: (t, 0, 0)),
                  pl.BlockSpec((1, LANE), lambda bi, t: (0, t))],
        out_specs=tok(s),
        out_shape=jax.ShapeDtypeStruct((b, s, n), BF16),
        scratch_shapes=[pltpu.VMEM((s // RET_CHUNK, RET_CHUNK, LANE), F32)],
        compiler_params=_cparams(("arbitrary", "arbitrary")),
        name="ret",
    )(rq, rk, rv, rg, rk_c, rv_c, eh, el, g_ret.reshape(1, n))


def _outproj_kernel(oa_ref, yr_ref, x_ref, mod_ref, wo_ref, gffn_ref, wr_ref, x1_ref, h2_ref, aff_ref):
    n_a = oa_ref.shape[2]
    tm = x_ref.shape[1]
    th = tm // OUTPROJ_SPLIT
    groups = [slice(g * th, (g + 1) * th) for g in range(OUTPROJ_SPLIT)]
    ys = [_dot(oa_ref[0, rows, :], wo_ref[0:n_a, :]) + _dot(yr_ref[0, rows, :], wo_ref[n_a:, :]) for rows in groups]
    h2s = []
    for g, (rows, y) in enumerate(zip(groups, ys)):
        x1 = x_ref[0, rows, :] + mod_ref[0, 2:3, :] * y
        x1_ref[0, rows, :] = x1
        h2 = _rms(x1) * gffn_ref[...] * (1.0 + mod_ref[0, 4:5, :]) + mod_ref[0, 3:4, :]
        for r in range(ROW_TILE):
            h2_ref[0, pl.ds(g * th * ROW_TILE + r, th, stride=ROW_TILE), :] = h2[:, r * LANE:(r + 1) * LANE]
        h2s.append(h2)
    n_e = aff_ref.shape[1]
    wr = wr_ref[...]
    w_hi = wr.astype(BF16)
    w_lo = (wr - w_hi.astype(F32)).astype(BF16)
    all_logits = []
    for h2 in h2s:
        h_hi = h2.astype(BF16)
        h_lo = (h2 - h_hi.astype(F32)).astype(BF16)
        lg = _dot(h_hi, w_hi) + (_dot(h_lo, w_hi) + _dot(h_hi, w_lo))
        all_logits.append(lg.T[:n_e])
    for rows, logits in zip(groups, all_logits):
        ex = jnp.exp(logits - jnp.max(logits, axis=0, keepdims=True))
        aff_ref[0, :, rows] = ex / jnp.sum(ex, axis=0, keepdims=True)


def _outproj(o_att, y_ret, x, mod, w_o, g_ffn, w_router):
    b, s, d = x.shape
    tm = min(s, 512)
    n_e = w_router.shape[1]
    assert n_e <= LANE
    w_router_t = jnp.pad(w_router, ((0, 0), (0, LANE - n_e)))
    tok = lambda n: pl.BlockSpec((1, tm, n), lambda bi, i: (bi, i, 0))
    full = lambda shape: pl.BlockSpec(shape, lambda bi, i: tuple(0 for _ in shape))
    return pl.pallas_call(
        _outproj_kernel,
        grid=(b, s // tm),
        in_specs=[tok(o_att.shape[2]), tok(y_ret.shape[2]), tok(d),
                  pl.BlockSpec((1, 6, d), lambda bi, i: (bi, 0, 0)),
                  full(w_o.shape), full((1, d)), full(w_router_t.shape)],
        out_specs=[tok(d), pl.BlockSpec((1, tm * ROW_TILE, LANE), lambda bi, i: (bi, i, 0)),
                   pl.BlockSpec((1, n_e, tm), lambda bi, i: (bi, 0, i))],
        out_shape=[jax.ShapeDtypeStruct((b, s, d), F32), jax.ShapeDtypeStruct((b, s * ROW_TILE, LANE), F32),
                   jax.ShapeDtypeStruct((b, n_e, s), F32)],
        compiler_params=_cparams(("arbitrary", "arbitrary")),
        name="outproj",
    )(o_att, y_ret, x, mod, w_o, g_ffn, w_router_t)


def _topk_kernel(aff_ref, idx_ref, c_ref, t_ref, *, cap):
    n_e, n_r, _ = c_ref.shape
    assert n_r <= LANE
    a = aff_ref[0]

    def count(mask):
        ones = jnp.where(mask, 1, 0)
        return jnp.sum(jnp.sum(ones, axis=1, keepdims=True), axis=2, keepdims=True)

    def as_float(bits):
        return pltpu.bitcast(bits, F32)

    def search(i, lo):
        cand = lo | jnp.left_shift(jnp.int32(1), 30 - i)
        return jnp.where(count(a >= as_float(cand)) >= cap, cand, lo)

    thr = lax.fori_loop(0, 31, search, jnp.zeros((n_e, 1, 1), jnp.int32))
    above = a >= as_float(thr + 1)
    need = cap - count(above)
    window = (a >= as_float(thr)) & jnp.logical_not(above)
    tok = (lax.broadcasted_iota(jnp.int32, (1, n_r, LANE), 1) * LANE
           + lax.broadcasted_iota(jnp.int32, (1, n_r, LANE), 2))

    def surplus(state):
        return jnp.max(state[1] - need) > 0

    def drop_one(state):
        win, n_win = state
        active = n_win > need
        inside = win > 0.5
        low = jnp.min(jnp.min(jnp.where(inside, a, jnp.inf), axis=1, keepdims=True), axis=2, keepdims=True)
        cand = inside & (a == low)
        last = jnp.max(jnp.max(jnp.where(cand, tok, -1), axis=1, keepdims=True), axis=2, keepdims=True)
        win = jnp.where(cand & (tok == last) & active, 0.0, win)
        return win, n_win - jnp.where(active, 1, 0)

    win, _ = lax.while_loop(surplus, drop_one, (jnp.where(window, 1.0, 0.0), count(window)))
    sel = above | (win > 0.5)

    li = lax.broadcasted_iota(jnp.int32, (LANE, LANE), 0)
    lj = lax.broadcasted_iota(jnp.int32, (LANE, LANE), 1)
    tri = jnp.where(li <= lj, 1.0, 0.0).astype(BF16)
    ones = jnp.ones((LANE, LANE), BF16)
    n_flat = n_e * n_r
    m2 = jnp.where(sel, 1.0, 0.0).astype(BF16).reshape(n_flat, LANE)
    within = _dot(m2, tri)
    tot = _dot(m2, ones)
    pi = lax.broadcasted_iota(jnp.int32, (n_flat, n_flat), 0)
    pj = lax.broadcasted_iota(jnp.int32, (n_flat, n_flat), 1)
    before = jnp.where((pi // n_r == pj // n_r) & (pj < pi), 1.0, 0.0).astype(BF16)
    off = _dot(before, tot.astype(BF16))
    c_ref[...] = (within + off).reshape(n_e, n_r, LANE)
    t_ref[...] = tot.reshape(n_e, n_r, LANE)

    incl = jnp.where(li <= lj, 1.0, 0.0).astype(BF16)[:n_r]
    jcol = lax.broadcasted_iota(jnp.int32, (cap, LANE), 0).astype(F32)
    lane = lax.broadcasted_iota(jnp.int32, (cap, LANE), 1)
    ones8 = jnp.ones((8, LANE), BF16)
    pad = jnp.zeros((LANE - n_r, LANE), F32)

    eye = jnp.where(li == lj, 1.0, 0.0).astype(BF16)[:n_r]

    def expert_group(i, _):
        es = [i * TOPK_GROUP + u for u in range(TOPK_GROUP)]
        tots = [t_ref[e].astype(BF16) for e in es]
        ends = [_dot_tn(t, incl)[0:1, :] for t in tots]
        starts = [end - _dot_tn(t, eye)[0:1, :] for t, end in zip(tots, ends)]
        in_rows = [(start <= jcol) & (end > jcol) & (lane < n_r) for start, end in zip(starts, ends)]
        c_pads = [jnp.concatenate([c_ref[e], pad], axis=0) if n_r < LANE else c_ref[e] for e in es]
        c_rows = [jnp.dot(jnp.where(m, 1.0, 0.0), c, preferred_element_type=F32, precision=HIGHEST)
                  for m, c in zip(in_rows, c_pads)]
        n_rows = [_dot_nt(ones8, jnp.where((end <= jcol) & (lane < n_r), 1.0, 0.0).astype(BF16)) for end in ends]
        n_lanes = [_dot_nt(ones8, jnp.where(c <= jcol, 1.0, 0.0).astype(BF16)) for c in c_rows]
        for e, nr, nl in zip(es, n_rows, n_lanes):
            idx_ref[0, pl.ds(e, 1), :] = (nr[0:1, :] * LANE + nl[0:1, :]).astype(jnp.int32)
        return 0

    assert n_e % TOPK_GROUP == 0
    lax.fori_loop(0, n_e // TOPK_GROUP, expert_group, 0)


def _topk(aff, cap):
    b, n_e, s = aff.shape
    n_r = s // LANE
    aff4 = aff.reshape(b, n_e, n_r, LANE)
    return pl.pallas_call(
        functools.partial(_topk_kernel, cap=cap),
        grid=(b,),
        in_specs=[pl.BlockSpec((1, n_e, n_r, LANE), lambda bi: (bi, 0, 0, 0))],
        out_specs=pl.BlockSpec((1, n_e, cap), lambda bi: (bi, 0, 0)),
        out_shape=jax.ShapeDtypeStruct((b, n_e, cap), jnp.int32),
        scratch_shapes=[pltpu.VMEM((n_e, n_r, LANE), F32), pltpu.VMEM((n_e, n_r, LANE), F32)],
        compiler_params=_cparams(("arbitrary",)),
        name="topk",
    )(aff4)


def _ffn_kernel(idx_ref, aff_ref, h2_hbm, wg_ref, wu_ref, wd_ref, out_hbm,
                h2_v, acc_v, sem, *part_bufs, tm):
    bi = pl.program_id(0)
    e = pl.program_id(1)
    cap = idx_ref.shape[2]
    n_parts = cap // tm
    xs_bufs, y_bufs = part_bufs[:n_parts], part_bufs[n_parts:]

    @pl.when(e == 0)
    def _():
        cp = pltpu.make_async_copy(h2_hbm.at[bi], h2_v, sem.at[0])
        cp.start()
        acc_v[...] = jnp.zeros_like(acc_v)
        cp.wait()

    def tile(t):
        return pl.ds(pl.multiple_of(t * ROW_TILE, ROW_TILE), ROW_TILE)

    def gather(part):
        for j in range(tm):
            xs_bufs[part][j * ROW_TILE:(j + 1) * ROW_TILE, :] = h2_v[tile(idx_ref[0, 0, part * tm + j]), :]

    def experts(part):
        xb = jnp.concatenate([xs_bufs[part][pl.ds(r, tm, stride=ROW_TILE), :] for r in range(ROW_TILE)],
                             axis=1).astype(BF16)
        a = _dot(xb, wg_ref[0])
        u = _dot(xb, wu_ref[0])
        y = _dot((_silu(a) * u).astype(BF16), wd_ref[0])
        for r in range(ROW_TILE):
            y_bufs[part][pl.ds(r, tm, stride=ROW_TILE), :] = y[:, r * LANE:(r + 1) * LANE]

    def scatter(part):
        for j0 in range(0, tm, FFN_UNROLL):
            toks = [idx_ref[0, 0, part * tm + j0 + u] for u in range(FFN_UNROLL)]
            rows = [acc_v[tile(t), :]
                    + y_bufs[part][(j0 + u) * ROW_TILE:(j0 + u + 1) * ROW_TILE, :] * aff_ref[0, 0, t]
                    for u, t in enumerate(toks)]
            for t, row in zip(toks, rows):
                acc_v[tile(t), :] = row

    gather(0)
    for part in range(n_parts):
        experts(part)
        if part + 1 < n_parts:
            gather(part + 1)
        scatter(part)

    @pl.when(e == pl.num_programs(1) - 1)
    def _():
        cp = pltpu.make_async_copy(acc_v, out_hbm.at[bi], sem.at[1])
        cp.start()
        cp.wait()


def _ffn(idx, aff, h2, w_gate, w_up, w_down):
    b, rows, _ = h2.shape
    s = rows // ROW_TILE
    n_e, d, f = w_gate.shape
    cap = idx.shape[2]
    tm = min(cap, 256)
    assert d == ROW_TILE * LANE and tm % FFN_UNROLL == 0
    smem = lambda n: pl.BlockSpec((1, 1, n), lambda bi, e: (bi * n_e + e, 0, 0), memory_space=pltpu.SMEM)
    idx = idx.reshape(b * n_e, 1, cap)
    aff = aff.reshape(b * n_e, 1, s)
    return pl.pallas_call(
        functools.partial(_ffn_kernel, tm=tm),
        grid=(b, n_e),
        in_specs=[smem(cap), smem(s),
                  pl.BlockSpec(memory_space=pl.ANY),
                  pl.BlockSpec((1, d, f), lambda bi, e: (e, 0, 0)),
                  pl.BlockSpec((1, d, f), lambda bi, e: (e, 0, 0)),
                  pl.BlockSpec((1, f, d), lambda bi, e: (e, 0, 0))],
        out_specs=pl.BlockSpec(memory_space=pl.ANY),
        out_shape=jax.ShapeDtypeStruct((b, rows, LANE), F32),
        scratch_shapes=[pltpu.VMEM((rows, LANE), F32), pltpu.VMEM((rows, LANE), F32),
                        pltpu.SemaphoreType.DMA((2,))]
        + [pltpu.VMEM((tm * ROW_TILE, LANE), F32)] * (2 * (cap // tm)),
        compiler_params=_cparams(("arbitrary", "arbitrary"), FFN_VMEM_LIMIT),
        name="ffn",
    )(idx, aff, h2, w_gate, w_up, w_down)


def _final_kernel(x1_ref, acc_ref, mod_ref, g_ref, o_ref):
    tm = x1_ref.shape[1]
    acc = jnp.concatenate([acc_ref[0, pl.ds(r, tm, stride=ROW_TILE), :] for r in range(ROW_TILE)], axis=1)
    x2 = x1_ref[0] + mod_ref[0, 5:6, :] * acc
    o_ref[0] = _rms(x2) * g_ref[...]


def _final(x1, acc, mod, g_final):
    b, s, d = x1.shape
    tm = min(s, 512)
    tok = pl.BlockSpec((1, tm, d), lambda bi, i: (bi, i, 0))
    return pl.pallas_call(
        _final_kernel,
        grid=(b, s // tm),
        in_specs=[tok, pl.BlockSpec((1, tm * ROW_TILE, LANE), lambda bi, i: (bi, i, 0)),
                  pl.BlockSpec((1, 6, d), lambda bi, i: (bi, 0, 0)),
                  pl.BlockSpec((1, d), lambda bi, i: (0, 0))],
        out_specs=tok,
        out_shape=jax.ShapeDtypeStruct((b, s, d), F32),
        compiler_params=_cparams(("arbitrary", "arbitrary")),
        name="final",
    )(x1, acc, mod, g_final.reshape(1, d))


def kernel(x, c, ctx, c_ctx, w_ada, b_ada, g_norm_mix, g_norm_ffn, w_in, g_q_lora, g_kv_lora, w_uq, w_ukv,
           ret_exp_fwd, ret_exp_bwd, g_ret, w_o, w_router, w_exp_gate, w_exp_up, w_exp_down, g_final):
    b, s, d = x.shape
    depth = w_ada.shape[0]
    assert depth == 1, "the context stream update is only needed for depth > 1"
    assert s % LANE == 0 and s % GRID_W == 0
    cap = EC_CAPACITY * s // N_EXPERTS

    rows = -(-(b + 1) // 8) * 8
    cc = jnp.zeros((rows, d), F32).at[:b].set(c).at[b].set(c_ctx)
    mods = _ada(cc, w_ada[0], b_ada[0])
    mod = mods[:b].reshape(b, 6, d)
    mod_c = mods[b:b + 1].reshape(1, 6, d)

    w_in_ext, w_uq_ext, w_ukv_ext = _prep_weights(w_in[0], w_uq[0], w_ukv[0])
    tabs = _rope_tables(s)
    gmix = g_norm_mix[0].reshape(1, d)
    gq = g_q_lora[0].reshape(1, -1)
    gkv = g_kv_lora[0].reshape(1, -1)

    qa, ka, va, rq, rk, rv, rg = _inproj(x, mod, gmix, w_in_ext, gq, gkv, w_uq_ext, w_ukv_ext, tabs, is_ctx=False)
    ka_c, va_c, rk_c, rv_c = _inproj(ctx, mod_c, gmix, w_in_ext, gq, gkv, w_uq_ext, w_ukv_ext, None, is_ctx=True)

    o_att = _attention(qa, ka, va, ka_c, va_c)
    y_ret = _retention(rq, rk, rv, rg, rk_c, rv_c, ret_exp_fwd[0], ret_exp_bwd[0], g_ret[0])

    x1, h2, aff = _outproj(o_att, y_ret, x, mod, w_o[0].astype(BF16), g_norm_ffn[0].reshape(1, d),
                           w_router[0])
    idx = _topk(aff, cap)
    acc = _ffn(idx, aff, h2, w_exp_gate[0].astype(BF16), w_exp_up[0].astype(BF16), w_exp_down[0].astype(BF16))
    return _final(x1, acc, mod, g_final)
```
